```python
import jax, jax.numpy as jnp
from jax import lax
import numpy as np

D_MODEL = 2048
BATCH = 1
SEQ = 8192
DEPTH = 1
DEC_BATCH = 4
DEC_SEQ = 8192
PAST_LEN = 128

MLA_HEADS = 8
QK_NOPE = 128
QK_ROPE = 64
V_HEAD = 128
Q_LORA = 512
KV_LORA = 512
MLA_WIDTH = MLA_HEADS * V_HEAD
RG_WIDTH = D_MODEL - MLA_WIDTH
RG_BLOCKS = 8
RG_BLOCK = RG_WIDTH // RG_BLOCKS
CONV_WIDTH = 4
CONV_LEFT = 1
LRU_C = 8.0
N_DIR = 2
IN_SPLITS = [Q_LORA, Q_LORA + KV_LORA, Q_LORA + KV_LORA + QK_ROPE, Q_LORA + KV_LORA + QK_ROPE + RG_WIDTH]
IN_WIDTH = Q_LORA + KV_LORA + QK_ROPE + 2 * RG_WIDTH
MEM_TOKENS = 256
X_HEADS = 4
X_HEAD = D_MODEL // X_HEADS
N_EXPERTS = 16
EC_CAPACITY_FACTOR = 2
EXPERT_FF = 1408

ROPE_THETA = 10000.0
EPS = 1e-6
Q_BLOCK = 128

kernel_name = "hymba_mla_birglru_ec_encoder"


def rms_norm(x, g):
    x32 = x.astype(jnp.float32)
    y = x32 * lax.rsqrt(jnp.mean(x32 * x32, axis=-1, keepdims=True) + EPS)
    return (y * g.astype(jnp.float32)).astype(x.dtype)


def rope_tables(S):
    inv = ROPE_THETA ** (-jnp.arange(0, QK_ROPE, 2, dtype=jnp.float32) / QK_ROPE)
    ang = jnp.arange(S, dtype=jnp.float32)[:, None] * inv[None, :]
    return jnp.cos(ang), jnp.sin(ang)


def apply_rope(x, cos, sin):
    x1, x2 = jnp.split(x, 2, axis=-1)
    return jnp.concatenate([x1 * cos - x2 * sin, x2 * cos + x1 * sin], axis=-1).astype(x.dtype)


def mla_mixer(c_q, c_kv, k_rope_raw, q_norm, w_uq, kv_norm, w_ukv):
    B, S, _ = c_q.shape
    q = (rms_norm(c_q, q_norm) @ w_uq).reshape(B, S, MLA_HEADS, QK_NOPE + QK_ROPE)
    q_nope, q_rope = q[..., :QK_NOPE], q[..., QK_NOPE:]
    kv = (rms_norm(c_kv, kv_norm) @ w_ukv).reshape(B, S, MLA_HEADS, QK_NOPE + V_HEAD)
    k_nope, v = kv[..., :QK_NOPE], kv[..., QK_NOPE:]
    cos, sin = rope_tables(S)
    q_rope = apply_rope(q_rope, cos[:, None, :], sin[:, None, :])
    k_rope = apply_rope(k_rope_raw, cos, sin)
    scale = (QK_NOPE + QK_ROPE) ** -0.5
    nb = S // Q_BLOCK

    def to_blocks(t):
        return jnp.moveaxis(t.reshape(B, nb, Q_BLOCK, *t.shape[2:]), 1, 0)

    def attend(blk):
        qn, qr = blk
        s = jnp.einsum('bqhd,bkhd->bhqk', qn, k_nope, preferred_element_type=jnp.float32)
        s = s + jnp.einsum('bqhr,bkr->bhqk', qr, k_rope, preferred_element_type=jnp.float32)
        p = jax.nn.softmax(s * scale, axis=-1).astype(v.dtype)
        return jnp.einsum('bhqk,bkhd->bqhd', p, v)

    o = lax.map(attend, (to_blocks(q_nope), to_blocks(q_rope)))
    return jnp.moveaxis(o, 0, 1).reshape(B, S, MLA_WIDTH)


def linear_scan(a, b, reverse):
    def combine(l, r):
        a1, b1 = l
        a2, b2 = r
        return a1 * a2, a2 * b1 + b2
    _, h = lax.associative_scan(combine, (a, b), axis=1, reverse=reverse)
    return h


def rglru_mixer(x_br, gate_br, conv_w, conv_b, w_a, b_a, w_x, b_x, lam):
    B, S, _ = x_br.shape
    xp = jnp.pad(x_br, ((0, 0), (CONV_LEFT, CONV_WIDTH - 1 - CONV_LEFT), (0, 0)))
    xc = conv_b + xp[:, 0:S] * conv_w[0]
    for k in range(1, CONV_WIDTH):
        xc = xc + xp[:, k:k + S] * conv_w[k]
    xb = xc.reshape(B, S, RG_BLOCKS, RG_BLOCK)
    r = jax.nn.sigmoid((jnp.einsum('bsgi,ngij->nbsgj', xb, w_a) + b_a[:, None, None]).astype(jnp.float32)).reshape(N_DIR, B, S, RG_WIDTH)
    i = jax.nn.sigmoid((jnp.einsum('bsgi,ngij->nbsgj', xb, w_x) + b_x[:, None, None]).astype(jnp.float32)).reshape(N_DIR, B, S, RG_WIDTH)
    log_a = LRU_C * r * jax.nn.log_sigmoid(lam.astype(jnp.float32))[:, None, None, :]
    a = jnp.exp(log_a)
    u = jnp.sqrt(-jnp.expm1(2.0 * log_a)) * i * xc.astype(jnp.float32)[None]
    h = linear_scan(a[0], u[0], False) + linear_scan(a[1], u[1], True)
    return h.astype(x_br.dtype) * jax.nn.gelu(gate_br, approximate=True)


def memory_cross_attention(hn, mem, norm_mem, w_cq, w_ck, w_cv, w_co):
    B, S, _ = hn.shape
    M = mem.shape[1]
    mn = rms_norm(mem, norm_mem)
    q = (hn @ w_cq).reshape(B, S, X_HEADS, X_HEAD)
    k = (mn @ w_ck).reshape(B, M, X_HEADS, X_HEAD)
    v = (mn @ w_cv).reshape(B, M, X_HEADS, X_HEAD)
    s = jnp.einsum('bqhd,bmhd->bhqm', q, k, preferred_element_type=jnp.float32) * (X_HEAD ** -0.5)
    p = jax.nn.softmax(s, axis=-1).astype(v.dtype)
    o = jnp.einsum('bhqm,bmhd->bqhd', p, v).reshape(B, S, D_MODEL)
    return o @ w_co


def expert_choice_moe(xn, w_router, w_gate, w_up, w_down):
    B, S, D = xn.shape
    n = B * S
    cap = EC_CAPACITY_FACTOR * n // N_EXPERTS
    tok = xn.reshape(n, D)
    aff = jax.nn.softmax((tok @ w_router).astype(jnp.float32), axis=-1)
    g, idx = lax.top_k(aff.T, cap)
    xe = tok[idx]
    hid = jax.nn.silu(jnp.einsum('ecd,edf->ecf', xe, w_gate)) * jnp.einsum('ecd,edf->ecf', xe, w_up)
    ye = jnp.einsum('ecf,efd->ecd', hid, w_down)
    contrib = (g[..., None].astype(ye.dtype) * ye).reshape(-1, D)
    out = jnp.zeros_like(tok).at[idx.reshape(-1)].add(contrib.astype(tok.dtype))
    return out.reshape(B, S, D)


def _normal(k, shape, fan_in):
    return jax.random.normal(k, shape, jnp.float32) * (fan_in ** -0.5)


def _gain(k, shape):
    return 1.0 + 0.02 * jax.random.normal(k, shape, jnp.float32)


def setup_inputs(seed: int = 0) -> dict:
    key = jax.random.key(seed)
    ks = jax.random.split(key, 32)
    L = DEPTH
    a0 = jax.random.uniform(ks[16], (L, N_DIR, RG_WIDTH), jnp.float32, minval=0.9, maxval=0.999)
    return {
        "x_prompt": jax.random.normal(ks[0], (BATCH, SEQ, D_MODEL), jnp.float32),
        "x_sample": jax.random.normal(ks[1], (DEC_BATCH, DEC_SEQ, D_MODEL), jnp.float32),
        "mem_prompt": jax.random.normal(ks[2], (BATCH, MEM_TOKENS, D_MODEL), jnp.float32),
        "mem_sample": jax.random.normal(ks[3], (DEC_BATCH, MEM_TOKENS, D_MODEL), jnp.float32),
        "norm_mix": _gain(ks[4], (L, D_MODEL)),
        "w_in": _normal(ks[5], (L, D_MODEL, IN_WIDTH), D_MODEL),
        "q_norm": _gain(ks[6], (L, Q_LORA)),
        "w_uq": _normal(ks[7], (L, Q_LORA, MLA_HEADS * (QK_NOPE + QK_ROPE)), Q_LORA),
        "kv_norm": _gain(ks[8], (L, KV_LORA)),
        "w_ukv": _normal(ks[9], (L, KV_LORA, MLA_HEADS * (QK_NOPE + V_HEAD)), KV_LORA),
        "conv_w": _normal(ks[10], (L, CONV_WIDTH, RG_WIDTH), CONV_WIDTH),
        "conv_b": 0.01 * jax.random.normal(ks[11], (L, RG_WIDTH), jnp.float32),
        "w_rg_a": _normal(ks[12], (L, N_DIR, RG_BLOCKS, RG_BLOCK, RG_BLOCK), RG_BLOCK),
        "b_rg_a": 0.01 * jax.random.normal(ks[13], (L, N_DIR, RG_BLOCKS, RG_BLOCK), jnp.float32),
        "w_rg_x": _normal(ks[14], (L, N_DIR, RG_BLOCKS, RG_BLOCK, RG_BLOCK), RG_BLOCK),
        "b_rg_x": 0.01 * jax.random.normal(ks[15], (L, N_DIR, RG_BLOCKS, RG_BLOCK), jnp.float32),
        "rg_lambda": jnp.log(a0) - jnp.log1p(-a0),
        "w_out": _normal(ks[17], (L, D_MODEL, D_MODEL), D_MODEL),
        "norm_cross": _gain(ks[18], (L, D_MODEL)),
        "norm_mem": _gain(ks[19], (L, D_MODEL)),
        "w_cq": _normal(ks[20], (L, D_MODEL, D_MODEL), D_MODEL),
        "w_ck": _normal(ks[21], (L, D_MODEL, D_MODEL), D_MODEL),
        "w_cv": _normal(ks[22], (L, D_MODEL, D_MODEL), D_MODEL),
        "w_co": _normal(ks[23], (L, D_MODEL, D_MODEL), D_MODEL),
        "norm_ffn": _gain(ks[24], (L, D_MODEL)),
        "w_router": _normal(ks[25], (L, D_MODEL, N_EXPERTS), D_MODEL),
        "w_gate": _normal(ks[26], (L, N_EXPERTS, D_MODEL, EXPERT_FF), D_MODEL),
        "w_up": _normal(ks[27], (L, N_EXPERTS, D_MODEL, EXPERT_FF), D_MODEL),
        "w_down": _normal(ks[28], (L, N_EXPERTS, EXPERT_FF, D_MODEL), EXPERT_FF),
        "norm_final": _gain(ks[29], (D_MODEL,)),
    }


def reference(x_prompt, x_sample, mem_prompt, mem_sample, norm_mix, w_in, q_norm, w_uq, kv_norm, w_ukv,
              conv_w, conv_b, w_rg_a, b_rg_a, w_rg_x, b_rg_x, rg_lambda, w_out, norm_cross, norm_mem,
              w_cq, w_ck, w_cv, w_co, norm_ffn, w_router, w_gate, w_up, w_down, norm_final):
    def run(h, mem):
        for l in range(DEPTH):
            hn = rms_norm(h, norm_mix[l])
            proj = hn @ w_in[l]
            c_q, c_kv, k_r, x_br, g_br = jnp.split(proj, IN_SPLITS, axis=-1)
            o_mla = mla_mixer(c_q, c_kv, k_r, q_norm[l], w_uq[l], kv_norm[l], w_ukv[l])
            o_rg = rglru_mixer(x_br, g_br, conv_w[l], conv_b[l], w_rg_a[l], b_rg_a[l], w_rg_x[l], b_rg_x[l], rg_lambda[l])
            h = h + jnp.concatenate([o_mla, o_rg], axis=-1) @ w_out[l]
            h = h + memory_cross_attention(rms_norm(h, norm_cross[l]), mem, norm_mem[l], w_cq[l], w_ck[l], w_cv[l], w_co[l])
            h = h + expert_choice_moe(rms_norm(h, norm_ffn[l]), w_router[l], w_gate[l], w_up[l], w_down[l])
        return rms_norm(h, norm_final)

    y_prompt = run(x_prompt, mem_prompt)
    y_sample = run(x_sample, mem_sample)
    return (y_prompt, y_sample)
```

```python
import functools

import jax
import jax.numpy as jnp
from jax import lax
from jax.experimental import pallas as pl
from jax.experimental.pallas import tpu as pltpu

F32 = jnp.float32
BF16 = jnp.bfloat16
I32 = jnp.int32

D_MODEL = 2048
MLA_HEADS = 8
QK_NOPE = 128
QK_ROPE = 64
V_HEAD = 128
Q_LORA = 512
KV_LORA = 512
MLA_WIDTH = MLA_HEADS * V_HEAD
RG_WIDTH = D_MODEL - MLA_WIDTH
RG_BLOCKS = 8
RG_BLOCK = RG_WIDTH // RG_BLOCKS
LRU_C = 8.0
MEM_TOKENS = 256
X_HEADS = 4
X_HEAD = D_MODEL // X_HEADS
N_EXPERTS = 16
EC_CAPACITY_FACTOR = 2
EXPERT_FF = 1408
ROPE_THETA = 10000.0
EPS = 1e-6

LANES = 128
HEAD_QK = 2 * LANES
VMEM_LIMIT = 56 * 1024 * 1024


def _cparams(sem, vmem=VMEM_LIMIT):
    return pltpu.CompilerParams(dimension_semantics=sem, vmem_limit_bytes=vmem)


def _rms(x, g):
    ms = jnp.mean(x * x, axis=-1, keepdims=True)
    return x * lax.rsqrt(ms + EPS) * g


def _dot(a, b):
    return jnp.dot(a, b, preferred_element_type=F32)


def _dot_nt(a, b):
    return lax.dot_general(a, b, (((1,), (1,)), ((), ())), preferred_element_type=F32)


def _inproj_kernel(x_ref, gmix_ref, win_ref, qg_ref, wuq_ref, kvg_ref, wukv_ref,
                   qmult_ref, kmult_ref, q_ref, k_ref, v_ref, xbr_ref, gbr_ref):
    hn = _rms(x_ref[0], gmix_ref[...]).astype(BF16)
    cq = _dot(hn, win_ref[:, 0:Q_LORA])
    ckv = _dot(hn, win_ref[:, Q_LORA:Q_LORA + KV_LORA])
    o = Q_LORA + KV_LORA
    xbr_ref[0] = _dot(hn, win_ref[:, o:o + RG_WIDTH])
    gbr_ref[0] = _dot(hn, win_ref[:, o + RG_WIDTH:o + 2 * RG_WIDTH])
    kr = _dot(hn, win_ref[:, o + 2 * RG_WIDTH:o + 2 * RG_WIDTH + LANES])
    tk = kr * kmult_ref[...]
    rkk = (tk + pltpu.roll(tk, LANES // 2, 1)).astype(BF16)

    q = _dot(_rms(cq, qg_ref[...]).astype(BF16), wuq_ref[...])
    kv = _dot(_rms(ckv, kvg_ref[...]).astype(BF16), wukv_ref[...])
    qm = qmult_ref[...]
    for h in range(MLA_HEADS):
        c = h * HEAD_QK
        q_ref[0, h] = (q[:, c:c + HEAD_QK] * qm).astype(BF16)
        k_ref[0, h, :, 0:LANES] = kv[:, c:c + LANES].astype(BF16)
        k_ref[0, h, :, LANES:HEAD_QK] = rkk
        v_ref[0, h] = kv[:, c + LANES:c + HEAD_QK].astype(BF16)


def _inproj(x, gmix, win, qg, wuq, kvg, wukv, qmult, kmult, tm):
    B, S, D = x.shape
    full = lambda a: pl.BlockSpec(a.shape, lambda b, i: (0,) * a.ndim)
    return pl.pallas_call(
        _inproj_kernel,
        grid=(B, S // tm),
        in_specs=[pl.BlockSpec((1, tm, D), lambda b, i: (b, i, 0)),
                  full(gmix), full(win), full(qg), full(wuq), full(kvg), full(wukv),
                  pl.BlockSpec((tm, HEAD_QK), lambda b, i: (i, 0)),
                  pl.BlockSpec((tm, LANES), lambda b, i: (i, 0))],
        out_specs=[pl.BlockSpec((1, MLA_HEADS, tm, HEAD_QK), lambda b, i: (b, 0, i, 0)),
                   pl.BlockSpec((1, MLA_HEADS, tm, HEAD_QK), lambda b, i: (b, 0, i, 0)),
                   pl.BlockSpec((1, MLA_HEADS, tm, V_HEAD), lambda b, i: (b, 0, i, 0)),
                   pl.BlockSpec((1, tm, RG_WIDTH), lambda b, i: (b, i, 0)),
                   pl.BlockSpec((1, tm, RG_WIDTH), lambda b, i: (b, i, 0))],
        out_shape=[jax.ShapeDtypeStruct((B, MLA_HEADS, S, HEAD_QK), BF16),
                   jax.ShapeDtypeStruct((B, MLA_HEADS, S, HEAD_QK), BF16),
                   jax.ShapeDtypeStruct((B, MLA_HEADS, S, V_HEAD), BF16),
                   jax.ShapeDtypeStruct((B, S, RG_WIDTH), F32),
                   jax.ShapeDtypeStruct((B, S, RG_WIDTH), F32)],
        compiler_params=_cparams(("parallel", "parallel")),
        name="inproj",
    )(x, gmix, win, qg, wuq, kvg, wukv, qmult, kmult)


def _attn_kernel(q_ref, k_ref, v_ref, o_ref, *, tk):
    q = q_ref[0, 0]
    tq = q.shape[0]
    nk = k_ref.shape[2] // tk

    def body(j, carry):
        m, l, acc = carry
        off = pl.multiple_of(j * tk, tk)
        s = _dot_nt(q, k_ref[0, 0, pl.ds(off, tk), :])
        m_new = jnp.maximum(m, jnp.max(s, axis=1, keepdims=True))
        alpha = jnp.exp(m - m_new)
        p = jnp.exp(s - m_new)
        l = alpha * l + jnp.sum(p, axis=1, keepdims=True)
        acc = alpha * acc + _dot(p.astype(BF16), v_ref[0, 0, pl.ds(off, tk), :])
        return m_new, l, acc

    init = (jnp.full((tq, 1), -jnp.inf, F32), jnp.zeros((tq, 1), F32), jnp.zeros((tq, V_HEAD), F32))
    _, l, acc = lax.fori_loop(0, nk, body, init)
    o_ref[0] = (acc / l).astype(BF16)


def _attention(q, k, v, tq, tk):
    B, H, S, _ = q.shape
    return pl.pallas_call(
        functools.partial(_attn_kernel, tk=tk),
        grid=(B, H, S // tq),
        in_specs=[pl.BlockSpec((1, 1, tq, HEAD_QK), lambda b, h, i: (b, h, i, 0)),
                  pl.BlockSpec((1, 1, S, HEAD_QK), lambda b, h, i: (b, h, 0, 0)),
                  pl.BlockSpec((1, 1, S, V_HEAD), lambda b, h, i: (b, h, 0, 0))],
        out_specs=pl.BlockSpec((1, tq, V_HEAD), lambda b, h, i: (b, i, h)),
        out_shape=jax.ShapeDtypeStruct((B, S, H * V_HEAD), BF16),
        compiler_params=_cparams(("parallel", "parallel", "parallel")),
        name="mla_attention",
    )(q, k, v)


def _rg_kernel(x_ref, g_ref, cw_ref, cb_ref, wg_ref, bg_ref, lam_ref, o_ref,
               hf_ref, ab_ref, ub_ref, *, R):
    S = x_ref.shape[1]
    nch = S // R
    W = RG_BLOCK
    cw = cw_ref[...]
    cb = cb_ref[...]
    ls = LRU_C * jax.nn.log_sigmoid(lam_ref[...])
    rows = lax.broadcasted_iota(I32, (R, W), 0)
    next_rows = R + 16

    def gates(c):
        r0 = pl.multiple_of(c * R, R)
        cur = x_ref[0, pl.ds(r0, R), :]
        prev8 = x_ref[0, pl.ds(pl.multiple_of(jnp.maximum(r0 - 8, 0), 8), 8), :]
        prev8 = jnp.where(c > 0, prev8, 0.0)
        next8 = x_ref[0, pl.ds(pl.multiple_of(jnp.minimum(r0 + R, S - 8), 8), 8), :]
        next8 = jnp.where(c < nch - 1, next8, 0.0)
        ext = jnp.concatenate([prev8, cur, next8], axis=0)
        xm1 = pltpu.roll(ext, 1, 0)[8:8 + R]
        xp1 = pltpu.roll(ext, next_rows - 1, 0)[8:8 + R]
        xp2 = pltpu.roll(ext, next_rows - 2, 0)[8:8 + R]
        xc = cb + xm1 * cw[0:1] + cur * cw[1:2] + xp1 * cw[2:3] + xp2 * cw[3:4]
        sg = jax.nn.sigmoid(_dot(xc.astype(BF16), wg_ref[0]) + bg_ref[0])
        out = []
        for n in range(2):
            log_a = sg[:, n * W:(n + 1) * W] * ls[n:n + 1]
            a = jnp.exp(log_a)
            t = jnp.tanh(log_a)
            one_minus_a2 = -2.0 * t / (1.0 - t)
            u = jnp.sqrt(one_minus_a2) * sg[:, (2 + n) * W:(3 + n) * W] * xc
            out.append((a, u))
        return out

    def scan(a, u, reverse):
        d = 1
        while d < R:
            if reverse:
                keep = rows < R - d
                sh = R - d
            else:
                keep = rows >= d
                sh = d
            a_s = jnp.where(keep, pltpu.roll(a, sh, 0), 1.0)
            u_s = jnp.where(keep, pltpu.roll(u, sh, 0), 0.0)
            u = a * u_s + u
            a = a * a_s
            d *= 2
        return a, u

    def fwd(c, carry):
        r0 = pl.multiple_of(c * R, R)
        (a_f, u_f), (a_b, u_b) = gates(c)
        ab_ref[pl.ds(r0, R), :] = a_b
        ub_ref[pl.ds(r0, R), :] = u_b
        acum, hloc = scan(a_f, u_f, False)
        h = acum * carry + hloc
        hf_ref[pl.ds(r0, R), :] = h
        return h[R - 1:R, :]

    lax.fori_loop(0, nch, fwd, jnp.zeros((1, W), F32))

    def bwd(i, carry):
        c = nch - 1 - i
        r0 = pl.multiple_of(c * R, R)
        acum, hloc = scan(ab_ref[pl.ds(r0, R), :], ub_ref[pl.ds(r0, R), :], True)
        h = acum * carry + hloc
        gate = jax.nn.gelu(g_ref[0, pl.ds(r0, R), :], approximate=True)
        o_ref[0, pl.ds(r0, R), :] = ((hf_ref[pl.ds(r0, R), :] + h) * gate).astype(BF16)
        return h[0:1, :]

    lax.fori_loop(0, nch, bwd, jnp.zeros((1, W), F32))


def _rglru(xbr, gbr, conv_w, conv_b, wgate, bgate, lam, R):
    B, S, _ = xbr.shape
    W = RG_BLOCK
    return pl.pallas_call(
        functools.partial(_rg_kernel, R=R),
        grid=(B, RG_BLOCKS),
        in_specs=[pl.BlockSpec((1, S, W), lambda b, g: (b, 0, g)),
                  pl.BlockSpec((1, S, W), lambda b, g: (b, 0, g)),
                  pl.BlockSpec((4, W), lambda b, g: (0, g)),
                  pl.BlockSpec((1, W), lambda b, g: (0, g)),
                  pl.BlockSpec((1, W, 4 * W), lambda b, g: (g, 0, 0)),
                  pl.BlockSpec((1, 1, 4 * W), lambda b, g: (g, 0, 0)),
                  pl.BlockSpec((2, W), lambda b, g: (0, g))],
        out_specs=pl.BlockSpec((1, S, W), lambda b, g: (b, 0, g)),
        out_shape=jax.ShapeDtypeStruct((B, S, RG_WIDTH), BF16),
        scratch_shapes=[pltpu.VMEM((S, W), F32), pltpu.VMEM((S, W), F32), pltpu.VMEM((S, W), F32)],
        compiler_params=_cparams(("parallel", "parallel")),
        name="rglru",
    )(xbr, gbr, conv_w, conv_b, wgate, bgate, lam)


def _outproj_kernel(om_ref, og_ref, x_ref, wout_ref, g_ref, h_ref, hn_ref):
    h = x_ref[...] + _dot(om_ref[...], wout_ref[0:MLA_WIDTH, :]) + _dot(og_ref[...], wout_ref[MLA_WIDTH:, :])
    h_ref[...] = h
    hn_ref[...] = _rms(h, g_ref[...]).astype(BF16)


def _outproj(o_mla, o_rg, x, wout, g, tm):
    M, D = x.shape
    return pl.pallas_call(
        _outproj_kernel,
        grid=(M // tm,),
        in_specs=[pl.BlockSpec((tm, MLA_WIDTH), lambda i: (i, 0)),
                  pl.BlockSpec((tm, RG_WIDTH), lambda i: (i, 0)),
                  pl.BlockSpec((tm, D), lambda i: (i, 0)),
                  pl.BlockSpec((D, D), lambda i: (0, 0)),
                  pl.BlockSpec((1, D), lambda i: (0, 0))],
        out_specs=[pl.BlockSpec((tm, D), lambda i: (i, 0)), pl.BlockSpec((tm, D), lambda i: (i, 0))],
        out_shape=[jax.ShapeDtypeStruct((M, D), F32), jax.ShapeDtypeStruct((M, D), BF16)],
        compiler_params=_cparams(("parallel",)),
        name="outproj",
    )(o_mla, o_rg, x, wout, g)


def _memkv_kernel(mem_ref, g_ref, wk_ref, wv_ref, k_ref, v_ref):
    mn = _rms(mem_ref[0], g_ref[...]).astype(BF16)
    k_ref[0] = _dot(mn, wk_ref[...]).astype(BF16)
    v_ref[0] = _dot(mn, wv_ref[...]).astype(BF16)


def _memkv(mem, g, wk, wv):
    B, Mt, D = mem.shape
    return pl.pallas_call(
        _memkv_kernel,
        grid=(B,),
        in_specs=[pl.BlockSpec((1, Mt, D), lambda b: (b, 0, 0)),
                  pl.BlockSpec((1, D), lambda b: (0, 0)),
                  pl.BlockSpec((D, D), lambda b: (0, 0)),
                  pl.BlockSpec((D, D), lambda b: (0, 0))],
        out_specs=[pl.BlockSpec((1, Mt, D), lambda b: (b, 0, 0)), pl.BlockSpec((1, Mt, D), lambda b: (b, 0, 0))],
        out_shape=[jax.ShapeDtypeStruct((B, Mt, D), BF16), jax.ShapeDtypeStruct((B, Mt, D), BF16)],
        compiler_params=_cparams(("parallel",)),
        name="memkv",
    )(mem, g, wk, wv)


def _cross_kernel(hn_ref, h_ref, k_ref, v_ref, wq_ref, wo_ref, g_ref, wr_ref, h2_ref, xn_ref, aff_ref):
    q = _dot(hn_ref[0], wq_ref[...]).astype(BF16)
    outs = []
    for hd in range(X_HEADS):
        c = hd * X_HEAD
        s = _dot_nt(q[:, c:c + X_HEAD], k_ref[0, :, c:c + X_HEAD]) * (X_HEAD ** -0.5)
        e = jnp.exp(s - jnp.max(s, axis=1, keepdims=True))
        p = e / jnp.sum(e, axis=1, keepdims=True)
        outs.append(_dot(p.astype(BF16), v_ref[0, :, c:c + X_HEAD]).astype(BF16))
    h2 = h_ref[0] + _dot(jnp.concatenate(outs, axis=1), wo_ref[...])
    h2_ref[0] = h2
    xn = _rms(h2, g_ref[...])
    xn_ref[0] = xn
    logits = _dot_nt(wr_ref[...], xn.astype(BF16))
    e = jnp.exp(logits - jnp.max(logits, axis=0, keepdims=True))
    aff_ref[...] = e / jnp.sum(e, axis=0, keepdims=True)


def _cross(hn, h1, kmem, vmem, wq, wo, g, wr_t, tm):
    B, S, D = h1.shape
    nt = S // tm
    return pl.pallas_call(
        _cross_kernel,
        grid=(B, nt),
        in_specs=[pl.BlockSpec((1, tm, D), lambda b, i: (b, i, 0)),
                  pl.BlockSpec((1, tm, D), lambda b, i: (b, i, 0)),
                  pl.BlockSpec((1, MEM_TOKENS, D), lambda b, i: (b, 0, 0)),
                  pl.BlockSpec((1, MEM_TOKENS, D), lambda b, i: (b, 0, 0)),
                  pl.BlockSpec((D, D), lambda b, i: (0, 0)),
                  pl.BlockSpec((D, D), lambda b, i: (0, 0)),
                  pl.BlockSpec((1, D), lambda b, i: (0, 0)),
                  pl.BlockSpec((N_EXPERTS, D), lambda b, i: (0, 0))],
        out_specs=[pl.BlockSpec((1, tm, D), lambda b, i: (b, i, 0)),
                   pl.BlockSpec((1, tm, D), lambda b, i: (b, i, 0)),
                   pl.BlockSpec((N_EXPERTS, tm), lambda b, i: (0, b * nt + i))],
        out_shape=[jax.ShapeDtypeStruct((B, S, D), F32),
                   jax.ShapeDtypeStruct((B, S, D), F32),
                   jax.ShapeDtypeStruct((N_EXPERTS, B * S), F32)],
        compiler_params=_cparams(("parallel", "parallel")),
        name="cross_router",
    )(hn, h1, kmem, vmem, wq, wo, g, wr_t)


def _select_kernel(aff_ref, idx_ref, cnt_ref, src_ref, w_ref, *, C, CB):
    nb = aff_ref.shape[1]
    L = LANES
    li = lax.broadcasted_iota(I32, (L, L), 0)
    lj = lax.broadcasted_iota(I32, (L, L), 1)
    upper_l = (li <= lj).astype(BF16)
    ones_l = jnp.ones((L, L), BF16)
    bi = lax.broadcasted_iota(I32, (nb, nb), 0)
    bj = lax.broadcasted_iota(I32, (nb, nb), 1)
    lower_strict = (bj < bi).astype(BF16)
    upper_b = (bi <= bj).astype(BF16)
    blk_id = lax.broadcasted_iota(I32, (nb, L), 0).astype(F32)
    lane = lax.broadcasted_iota(I32, (CB, L), 1)

    cnt_ref[...] = jnp.zeros_like(cnt_ref)
    src_ref[...] = jnp.zeros_like(src_ref)
    w_ref[...] = jnp.zeros_like(w_ref)
    idx_ref[...] = jnp.zeros_like(idx_ref)

    def count(m):
        return jnp.sum(jnp.sum(m.astype(I32), axis=1, keepdims=True), axis=0, keepdims=True)

    def prefix(m_b):
        within = _dot(m_b, upper_l)
        tot = _dot(m_b, ones_l)
        t_excl = _dot(lower_strict, tot.astype(BF16))
        return within, tot, t_excl

    def per_expert(e, _):
        a = aff_ref[e]
        bits = pltpu.bitcast(a, I32)

        def bisect(i, thr):
            cand = thr | jnp.left_shift(jnp.int32(1), 30 - i)
            return jnp.where(count(bits >= cand) >= C, cand, thr)

        thr = lax.fori_loop(0, 31, bisect, jnp.zeros((1, 1), I32))
        gt = bits > thr
        eq = bits == thr
        need = (C - count(gt)).astype(F32)
        e_within, _, e_excl = prefix(eq.astype(BF16))
        eq_rank = e_within - eq.astype(F32) + e_excl
        mask = gt | (eq & (eq_rank < need))
        mask_f = mask.astype(F32)
        within, tot, t_excl = prefix(mask.astype(BF16))
        pos = (within - mask_f + t_excl).astype(I32)

        rank = cnt_ref[...]
        flat = e * C + pos
        for r in range(N_EXPERTS):
            sel = mask & (rank == r)
            src_ref[r] = jnp.where(sel, flat, src_ref[r])
            w_ref[r] = jnp.where(sel, a, w_ref[r])
        cnt_ref[...] = rank + mask.astype(I32)

        tot_row = _dot_nt(ones_l, mask.astype(BF16))
        t_incl_row = _dot(tot_row.astype(BF16), upper_b)[0:1, :]
        t_excl_row = t_incl_row - tot_row[0:1, :]
        hi = jnp.floor(t_excl * (1.0 / 256.0))
        rhs = jnp.concatenate([within, hi, t_excl - 256.0 * hi, blk_id], axis=1).astype(BF16)

        def per_chunk(ci, _):
            c0 = pl.multiple_of(ci * CB, CB)
            c_b = (c0 + lax.broadcasted_iota(I32, (CB, nb), 0)).astype(F32)
            onehot = ((t_excl_row <= c_b) & (c_b < t_incl_row)).astype(BF16)
            g = _dot(onehot, rhs)
            c_l = (c0 + lax.broadcasted_iota(I32, (CB, L), 0)).astype(F32)
            inside = (g[:, 0:L] + (256.0 * g[:, L:2 * L] + g[:, 2 * L:3 * L])) <= c_l
            val = (g[:, 3 * L:4 * L] * float(L) + _dot(inside.astype(BF16), ones_l)).astype(I32)
            idx_ref[pl.ds(c0, CB), :] = jnp.where(lane == e, val, idx_ref[pl.ds(c0, CB), :])
            return 0

        lax.fori_loop(0, C // CB, per_chunk, 0)
        return 0

    lax.fori_loop(0, N_EXPERTS, per_expert, 0)


def _select(aff3, C):
    E, nb, L = aff3.shape
    CB = min(C, 512)
    return pl.pallas_call(
        functools.partial(_select_kernel, C=C, CB=CB),
        out_shape=[jax.ShapeDtypeStruct((C, L), I32),
                   jax.ShapeDtypeStruct((nb, L), I32),
                   jax.ShapeDtypeStruct((E, nb, L), I32),
                   jax.ShapeDtypeStruct((E, nb, L), F32)],
        compiler_params=pltpu.CompilerParams(vmem_limit_bytes=VMEM_LIMIT),
        name="ec_select",
    )(aff3)


GATHER_CHUNK = 32


def _gather_kernel(idx_ref, x_ref, o_ref, sem):
    nrows = idx_ref.shape[0]

    def row_copy(src_row, dst_row):
        return pltpu.make_async_copy(x_ref.at[pl.ds(src_row, 1)], o_ref.at[pl.ds(dst_row, 1)], sem)

    def drain():
        for _ in range(GATHER_CHUNK):
            row_copy(0, 0).wait()

    def body(c, _):
        base = c * GATHER_CHUNK
        for u in range(GATHER_CHUNK):
            row_copy(idx_ref[base + u], base + u).start()

        @pl.when(c > 0)
        def _():
            drain()
        return 0

    lax.fori_loop(0, nrows // GATHER_CHUNK, body, 0)
    drain()


def _gather(idx, xn):
    nrows = idx.shape[0]
    return pl.pallas_call(
        _gather_kernel,
        in_specs=[pl.BlockSpec(memory_space=pltpu.SMEM), pl.BlockSpec(memory_space=pl.ANY)],
        out_specs=pl.BlockSpec(memory_space=pl.ANY),
        out_shape=jax.ShapeDtypeStruct((nrows, xn.shape[1]), xn.dtype),
        scratch_shapes=[pltpu.SemaphoreType.DMA],
        name="ec_gather",
    )(idx, xn)


def _ffn_kernel(x_ref, wg_ref, wu_ref, wd_ref, o_ref):
    x = x_ref[...].astype(BF16)
    hid = jax.nn.silu(_dot(x, wg_ref[0])) * _dot(x, wu_ref[0])
    o_ref[...] = _dot(hid.astype(BF16), wd_ref[0])


def _ffn(xe, wg, wu, wd, C, tc):
    E, D, F = wg.shape
    nt = C // tc
    return pl.pallas_call(
        _ffn_kernel,
        grid=(E, nt),
        in_specs=[pl.BlockSpec((tc, D), lambda e, i: (e * nt + i, 0)),
                  pl.BlockSpec((1, D, F), lambda e, i: (e, 0, 0)),
                  pl.BlockSpec((1, D, F), lambda e, i: (e, 0, 0)),
                  pl.BlockSpec((1, F, D), lambda e, i: (e, 0, 0))],
        out_specs=pl.BlockSpec((tc, D), lambda e, i: (e * nt + i, 0)),
        out_shape=jax.ShapeDtypeStruct((E * C, D), F32),
        compiler_params=_cparams(("parallel", "parallel")),
        name="ec_ffn",
    )(xe, wg, wu, wd)


def _combine_kernel(cnt_hbm, src_hbm, w_ref, h_ref, ye_hbm, g_ref, o_ref,
                    cnt_s, src_s, buf, acc_ref, sem_idx, sem_rows, *, tm):
    i = pl.program_id(0)
    c1 = pltpu.make_async_copy(cnt_hbm.at[i], cnt_s, sem_idx.at[0])
    c2 = pltpu.make_async_copy(src_hbm.at[i], src_s, sem_idx.at[1])
    c1.start()
    c2.start()
    c1.wait()
    c2.wait()

    def row_copy(src_row, r, t):
        return pltpu.make_async_copy(ye_hbm.at[pl.ds(src_row, 1)], buf.at[r, pl.ds(t, 1)], sem_rows)

    def per_token(t, carry):
        total, cmax = carry
        c = cnt_s[t]

        def per_entry(r, _):
            row_copy(src_s[r * tm + t], r, t).start()
            return 0

        lax.fori_loop(0, c, per_entry, 0)
        return total + c, jnp.maximum(cmax, c)

    total, cmax = lax.fori_loop(0, tm, per_token, (jnp.int32(0), jnp.int32(0)))

    def drain(_, c):
        row_copy(0, 0, 0).wait()
        return c

    lax.fori_loop(0, total, drain, 0)

    acc_ref[...] = h_ref[...]
    w = w_ref[...]
    cnt_col = w[:, N_EXPERTS:N_EXPERTS + 1]
    for r in range(N_EXPERTS):
        @pl.when(r < cmax)
        def _():
            acc_ref[...] += jnp.where(cnt_col > r, w[:, r:r + 1] * buf[r], 0.0)
    o_ref[...] = _rms(acc_ref[...], g_ref[...])


def _combine(cnt_t, src_t, winfo, h2, ye, g, tm):
    n, D = h2.shape
    nt = n // tm
    return pl.pallas_call(
        functools.partial(_combine_kernel, tm=tm),
        grid=(nt,),
        in_specs=[pl.BlockSpec(memory_space=pl.ANY),
                  pl.BlockSpec(memory_space=pl.ANY),
                  pl.BlockSpec((tm, 2 * N_EXPERTS), lambda i: (i, 0)),
                  pl.BlockSpec((tm, D), lambda i: (i, 0)),
                  pl.BlockSpec(memory_space=pl.ANY),
                  pl.BlockSpec((1, D), lambda i: (0, 0))],
        out_specs=pl.BlockSpec((tm, D), lambda i: (i, 0)),
        out_shape=jax.ShapeDtypeStruct((n, D), F32),
        scratch_shapes=[pltpu.SMEM((tm,), I32),
                        pltpu.SMEM((N_EXPERTS * tm,), I32),
                        pltpu.VMEM((N_EXPERTS, tm, D), F32),
                        pltpu.VMEM((tm, D), F32),
                        pltpu.SemaphoreType.DMA((2,)),
                        pltpu.SemaphoreType.DMA],
        compiler_params=_cparams(("arbitrary",)),
        name="ec_combine",
    )(cnt_t, src_t, winfo, h2, ye, g)


def _pick(n, prefs):
    for p in prefs:
        if n % p == 0:
            return p
    raise ValueError(f"no tile in {prefs} divides {n}")


def _prep_weights(norm_mix, w_in, q_norm, w_uq, kv_norm, w_ukv, conv_w, conv_b, w_rg_a, b_rg_a, w_rg_x, b_rg_x,
                  rg_lambda, w_out, norm_cross, norm_mem, w_cq, w_ck, w_cv, w_co, norm_ffn, w_router,
                  w_gate, w_up, w_down, norm_final):
    swap = (jnp.arange(QK_ROPE) + QK_ROPE // 2) % QK_ROPE
    o = Q_LORA + KV_LORA
    k_r = w_in[:, o:o + QK_ROPE]
    win = jnp.concatenate([w_in[:, :o], w_in[:, o + QK_ROPE:], k_r, k_r[:, swap]], axis=1).astype(BF16)
    uq = w_uq.reshape(Q_LORA, MLA_HEADS, QK_NOPE + QK_ROPE)
    rope = uq[:, :, QK_NOPE:]
    wuq = jnp.concatenate([uq[:, :, :QK_NOPE], rope, rope[:, :, swap]], axis=2).reshape(Q_LORA, MLA_HEADS * HEAD_QK)
    wgate = jnp.concatenate([w_rg_a[0], w_rg_a[1], w_rg_x[0], w_rg_x[1]], axis=2).astype(BF16)
    bgate = jnp.concatenate([b_rg_a[0], b_rg_a[1], b_rg_x[0], b_rg_x[1]], axis=1)[:, None, :]
    row = lambda v: v.reshape(1, -1)
    return dict(
        gmix=row(norm_mix), win=win, qg=row(q_norm), wuq=wuq.astype(BF16), kvg=row(kv_norm), wukv=w_ukv.astype(BF16),
        conv_w=conv_w, conv_b=row(conv_b), wgate=wgate, bgate=bgate, lam=rg_lambda,
        wout=w_out.astype(BF16), gcross=row(norm_cross), gmem=row(norm_mem),
        wcq=w_cq.astype(BF16), wck=w_ck.astype(BF16), wcv=w_cv.astype(BF16), wco=w_co.astype(BF16),
        gffn=row(norm_ffn), wr_t=w_router.T.astype(BF16),
        wg=w_gate.astype(BF16), wu=w_up.astype(BF16), wd=w_down.astype(BF16), gfinal=row(norm_final))


def _rope_mults(S):
    inv = ROPE_THETA ** (-jnp.arange(0, QK_ROPE, 2, dtype=F32) / QK_ROPE)
    ang = jnp.arange(S, dtype=F32)[:, None] * inv[None, :]
    cos, sin = jnp.cos(ang), jnp.sin(ang)
    kmult = jnp.concatenate([cos, cos, -sin, sin], axis=1)
    scale = (QK_NOPE + QK_ROPE) ** -0.5
    qmult = scale * jnp.concatenate([jnp.ones((S, QK_NOPE), F32), kmult], axis=1)
    return qmult, kmult


def _run(x, mem, W):
    B, S, D = x.shape
    n = B * S
    C = EC_CAPACITY_FACTOR * n // N_EXPERTS
    qmult, kmult = _rope_mults(S)

    q, k, v, xbr, gbr = _inproj(x, W["gmix"], W["win"], W["qg"], W["wuq"], W["kvg"], W["wukv"], qmult, kmult,
                                tm=_pick(S, (256, 128)))
    o_mla = _attention(q, k, v, tq=_pick(S, (512, 256, 128)), tk=_pick(S, (512, 256, 128)))
    o_rg = _rglru(xbr, gbr, W["conv_w"], W["conv_b"], W["wgate"], W["bgate"], W["lam"], R=_pick(S, (256, 128)))
    h1, hn = _outproj(o_mla.reshape(n, MLA_WIDTH), o_rg.reshape(n, RG_WIDTH), x.reshape(n, D), W["wout"],
                      W["gcross"], tm=_pick(n, (512, 256, 128)))
    kmem, vmem = _memkv(mem, W["gmem"], W["wck"], W["wcv"])
    h2, xn, aff_t = _cross(hn.reshape(B, S, D), h1.reshape(B, S, D), kmem, vmem, W["wcq"], W["wco"], W["gffn"],
                           W["wr_t"], tm=_pick(S, (256, 128)))

    nb = max(n // LANES, LANES)
    aff3 = jnp.pad(aff_t, ((0, 0), (0, nb * LANES - n)), constant_values=-1.0).reshape(N_EXPERTS, nb, LANES)
    idx_t, cnt, src, w = _select(aff3, C)
    idx = idx_t[:, :N_EXPERTS].T.reshape(N_EXPERTS * C)
    xe = _gather(idx, xn.reshape(n, D))
    ye = _ffn(xe, W["wg"], W["wu"], W["wd"], C, tc=_pick(C, (256, 128)))

    tm = LANES
    nt = n // tm
    cnt_flat = cnt.reshape(nb * LANES)[:n]
    src_t = src.reshape(N_EXPERTS, nb * LANES)[:, :n].reshape(N_EXPERTS, nt, tm).transpose(1, 0, 2)
    w_t = w.reshape(N_EXPERTS, nb * LANES)[:, :n].T
    winfo = jnp.concatenate([w_t, jnp.broadcast_to(cnt_flat.astype(F32)[:, None], (n, N_EXPERTS))], axis=1)
    y = _combine(cnt_flat.reshape(nt, tm), src_t.reshape(nt, N_EXPERTS * tm), winfo, h2.reshape(n, D), ye,
                 W["gfinal"], tm=tm)
    return y.reshape(B, S, D)


def kernel(x_prompt, x_sample, mem_prompt, mem_sample, norm_mix, w_in, q_norm, w_uq, kv_norm, w_ukv, conv_w, conv_b,
           w_rg_a, b_rg_a, w_rg_x, b_rg_x, rg_lambda, w_out, norm_cross, norm_mem, w_cq, w_ck, w_cv, w_co, norm_ffn,
           w_router, w_gate, w_up, w_down, norm_final):
    W = _prep_weights(norm_mix[0], w_in[0], q_norm[0], w_uq[0], kv_norm[0], w_ukv[0], conv_w[0], conv_b[0],
                      w_rg_a[0], b_rg_a[0], w_rg_x[0], b_rg_x[0], rg_lambda[0], w_out[0], norm_cross[0],
                      norm_mem[0], w_cq[0], w_ck[0], w_cv[0], w_co[0], norm_ffn[0], w_router[0],
                      w_gate[0], w_up[0], w_down[0], norm_final)
    return (_run(x_prompt, mem_prompt, W), _run(x_sample, mem_sample, W))
```

```python
import functools

import jax
import jax.numpy as jnp
from jax import lax
from jax.experimental import pallas as pl
from jax.experimental.pallas import tpu as pltpu

F32 = jnp.float32
BF16 = jnp.bfloat16
I32 = jnp.int32

D_MODEL = 2048
MLA_HEADS = 8
QK_NOPE = 128
QK_ROPE = 64
V_HEAD = 128
Q_LORA = 512
KV_LORA = 512
MLA_WIDTH = MLA_HEADS * V_HEAD
RG_WIDTH = D_MODEL - MLA_WIDTH
RG_BLOCKS = 8
RG_BLOCK = RG_WIDTH // RG_BLOCKS
LRU_C = 8.0
MEM_TOKENS = 256
X_HEADS = 4
X_HEAD = D_MODEL // X_HEADS
N_EXPERTS = 16
EC_CAPACITY_FACTOR = 2
EXPERT_FF = 1408
ROPE_THETA = 10000.0
EPS = 1e-6

LANES = 128
HEAD_QK = 2 * LANES
VMEM_LIMIT = 56 * 1024 * 1024


def _cparams(sem, vmem=VMEM_LIMIT):
    return pltpu.CompilerParams(dimension_semantics=sem, vmem_limit_bytes=vmem)


def _rms(x, g):
    ms = jnp.mean(x * x, axis=-1, keepdims=True)
    return x * lax.rsqrt(ms + EPS) * g


def _dot(a, b):
    return jnp.dot(a, b, preferred_element_type=F32)


def _dot_nt(a, b):
    return lax.dot_general(a, b, (((1,), (1,)), ((), ())), preferred_element_type=F32)


SLAB = D_MODEL // (2 * LANES)
HI_MASK = -65536


def _store_slabs(ref, row0, m, y):
    for j in range(SLAB):
        lo = pltpu.bitcast(y[:, j * LANES:(j + 1) * LANES].astype(BF16).astype(F32), I32)
        hi = pltpu.bitcast(y[:, (j + SLAB) * LANES:(j + SLAB + 1) * LANES].astype(BF16).astype(F32), I32)
        ref[pl.ds(row0 * SLAB + j, m, stride=SLAB), :] = (hi & HI_MASK) | lax.shift_right_logical(lo, 16)


def _load_slabs(ref, row0, m):
    lo, hi = [], []
    for j in range(SLAB):
        w = ref[pl.ds(row0 * SLAB + j, m, stride=SLAB), :]
        lo.append(pltpu.bitcast(lax.shift_left(w, 16), F32).astype(BF16))
        hi.append(pltpu.bitcast(w & HI_MASK, F32).astype(BF16))
    return jnp.concatenate(lo + hi, axis=1)


def _inproj_kernel(x_ref, gmix_ref, win_ref, qg_ref, wuq_ref, kvg_ref, wukv_ref,
                   qmult_ref, kmult_ref, q_ref, k_ref, v_ref, xbr_ref, gbr_ref):
    hn = _rms(x_ref[0], gmix_ref[...]).astype(BF16)
    cq = _dot(hn, win_ref[:, 0:Q_LORA])
    ckv = _dot(hn, win_ref[:, Q_LORA:Q_LORA + KV_LORA])
    o = Q_LORA + KV_LORA
    xbr_ref[0] = _dot(hn, win_ref[:, o:o + RG_WIDTH])
    gbr_ref[0] = _dot(hn, win_ref[:, o + RG_WIDTH:o + 2 * RG_WIDTH])
    kr = _dot(hn, win_ref[:, o + 2 * RG_WIDTH:o + 2 * RG_WIDTH + LANES])
    tk = kr * kmult_ref[...]
    rkk = (tk + pltpu.roll(tk, LANES // 2, 1)).astype(BF16)

    q = _dot(_rms(cq, qg_ref[...]).astype(BF16), wuq_ref[...])
    kv = _dot(_rms(ckv, kvg_ref[...]).astype(BF16), wukv_ref[...])
    qm = qmult_ref[...]
    for h in range(MLA_HEADS):
        c = h * HEAD_QK
        q_ref[0, h] = (q[:, c:c + HEAD_QK] * qm).astype(BF16)
        k_ref[0, h, :, 0:LANES] = kv[:, c:c + LANES].astype(BF16)
        k_ref[0, h, :, LANES:HEAD_QK] = rkk
        v_ref[0, h, :, 0:LANES] = kv[:, c + LANES:c + HEAD_QK].astype(BF16)
        v_ref[0, h, :, LANES:HEAD_QK] = jnp.ones((kv.shape[0], LANES), BF16)


def _inproj(x, gmix, win, qg, wuq, kvg, wukv, qmult, kmult, tm):
    B, S, D = x.shape
    full = lambda a: pl.BlockSpec(a.shape, lambda b, i: (0,) * a.ndim)
    return pl.pallas_call(
        _inproj_kernel,
        grid=(B, S // tm),
        in_specs=[pl.BlockSpec((1, tm, D), lambda b, i: (b, i, 0)),
                  full(gmix), full(win), full(qg), full(wuq), full(kvg), full(wukv),
                  pl.BlockSpec((tm, HEAD_QK), lambda b, i: (i, 0)),
                  pl.BlockSpec((tm, LANES), lambda b, i: (i, 0))],
        out_specs=[pl.BlockSpec((1, MLA_HEADS, tm, HEAD_QK), lambda b, i: (b, 0, i, 0)),
                   pl.BlockSpec((1, MLA_HEADS, tm, HEAD_QK), lambda b, i: (b, 0, i, 0)),
                   pl.BlockSpec((1, MLA_HEADS, tm, 2 * V_HEAD), lambda b, i: (b, 0, i, 0)),
                   pl.BlockSpec((1, tm, RG_WIDTH), lambda b, i: (b, i, 0)),
                   pl.BlockSpec((1, tm, RG_WIDTH), lambda b, i: (b, i, 0))],
        out_shape=[jax.ShapeDtypeStruct((B, MLA_HEADS, S, HEAD_QK), BF16),
                   jax.ShapeDtypeStruct((B, MLA_HEADS, S, HEAD_QK), BF16),
                   jax.ShapeDtypeStruct((B, MLA_HEADS, S, 2 * V_HEAD), BF16),
                   jax.ShapeDtypeStruct((B, S, RG_WIDTH), F32),
                   jax.ShapeDtypeStruct((B, S, RG_WIDTH), F32)],
        compiler_params=_cparams(("parallel", "parallel")),
        name="inproj",
    )(x, gmix, win, qg, wuq, kvg, wukv, qmult, kmult)


def _attn_kernel(q_ref, k_ref, v_ref, o_ref, *, tk):
    q = q_ref[0, 0]
    tq = q.shape[0]
    nk = k_ref.shape[2] // tk

    def body(j, carry):
        m, acc = carry
        off = pl.multiple_of(j * tk, tk)
        s = _dot_nt(q, k_ref[0, 0, pl.ds(off, tk), :])
        m_new = jnp.maximum(m, jnp.max(s, axis=1, keepdims=True))
        alpha = jnp.exp2(m - m_new)
        p = jnp.exp2(s - m_new)
        acc = alpha * acc + _dot(p.astype(BF16), v_ref[0, 0, pl.ds(off, tk), :])
        return m_new, acc

    init = (jnp.full((tq, 1), -jnp.inf, F32), jnp.zeros((tq, 2 * V_HEAD), F32))
    _, acc = lax.fori_loop(0, nk, body, init, unroll=8)
    o_ref[0] = (acc[:, :V_HEAD] / acc[:, V_HEAD:]).astype(BF16)


def _attention(q, k, v, tq, tk):
    B, H, S, _ = q.shape
    return pl.pallas_call(
        functools.partial(_attn_kernel, tk=tk),
        grid=(B, H, S // tq),
        in_specs=[pl.BlockSpec((1, 1, tq, HEAD_QK), lambda b, h, i: (b, h, i, 0)),
                  pl.BlockSpec((1, 1, S, HEAD_QK), lambda b, h, i: (b, h, 0, 0)),
                  pl.BlockSpec((1, 1, S, 2 * V_HEAD), lambda b, h, i: (b, h, 0, 0))],
        out_specs=pl.BlockSpec((1, tq, V_HEAD), lambda b, h, i: (b, i, h)),
        out_shape=jax.ShapeDtypeStruct((B, S, H * V_HEAD), BF16),
        compiler_params=_cparams(("parallel", "parallel", "parallel")),
        name="mla_attention",
    )(q, k, v)


def _rg_kernel(x_ref, g_ref, cw_ref, cb_ref, wg_ref, bg_ref, lam_ref, o_ref,
               hf_ref, ab_ref, ub_ref, *, R):
    S = x_ref.shape[1]
    nch = S // R
    W = RG_BLOCK
    cw = cw_ref[...]
    cb = cb_ref[...]
    ls = LRU_C * jax.nn.log_sigmoid(lam_ref[...])
    rows = lax.broadcasted_iota(I32, (R, W), 0)
    next_rows = R + 16

    def gates(c):
        r0 = pl.multiple_of(c * R, R)
        cur = x_ref[0, pl.ds(r0, R), :]
        prev8 = x_ref[0, pl.ds(pl.multiple_of(jnp.maximum(r0 - 8, 0), 8), 8), :]
        prev8 = jnp.where(c > 0, prev8, 0.0)
        next8 = x_ref[0, pl.ds(pl.multiple_of(jnp.minimum(r0 + R, S - 8), 8), 8), :]
        next8 = jnp.where(c < nch - 1, next8, 0.0)
        ext = jnp.concatenate([prev8, cur, next8], axis=0)
        xm1 = pltpu.roll(ext, 1, 0)[8:8 + R]
        xp1 = pltpu.roll(ext, next_rows - 1, 0)[8:8 + R]
        xp2 = pltpu.roll(ext, next_rows - 2, 0)[8:8 + R]
        xc = cb + xm1 * cw[0:1] + cur * cw[1:2] + xp1 * cw[2:3] + xp2 * cw[3:4]
        sg = jax.nn.sigmoid(_dot(xc.astype(BF16), wg_ref[0]) + bg_ref[0])
        out = []
        for n in range(2):
            log_a = sg[:, n * W:(n + 1) * W] * ls[n:n + 1]
            a = jnp.exp(log_a)
            t = jnp.tanh(log_a)
            one_minus_a2 = -2.0 * t / (1.0 - t)
            u = jnp.sqrt(one_minus_a2) * sg[:, (2 + n) * W:(3 + n) * W] * xc
            out.append((a, u))
        return out

    def scan(a, u, reverse):
        d = 1
        while d < R:
            if reverse:
                keep = rows < R - d
                sh = R - d
            else:
                keep = rows >= d
                sh = d
            a_s = jnp.where(keep, pltpu.roll(a, sh, 0), 1.0)
            u_s = jnp.where(keep, pltpu.roll(u, sh, 0), 0.0)
            u = a * u_s + u
            a = a * a_s
            d *= 2
        return a, u

    def fwd(c, carry):
        r0 = pl.multiple_of(c * R, R)
        (a_f, u_f), (a_b, u_b) = gates(c)
        ab_ref[pl.ds(r0, R), :] = a_b
        ub_ref[pl.ds(r0, R), :] = u_b
        acum, hloc = scan(a_f, u_f, False)
        h = acum * carry + hloc
        hf_ref[pl.ds(r0, R), :] = h
        return h[R - 1:R, :]

    lax.fori_loop(0, nch, fwd, jnp.zeros((1, W), F32))

    def bwd(i, carry):
        c = nch - 1 - i
        r0 = pl.multiple_of(c * R, R)
        acum, hloc = scan(ab_ref[pl.ds(r0, R), :], ub_ref[pl.ds(r0, R), :], True)
        h = acum * carry + hloc
        gate = jax.nn.gelu(g_ref[0, pl.ds(r0, R), :], approximate=True)
        o_ref[0, pl.ds(r0, R), :] = ((hf_ref[pl.ds(r0, R), :] + h) * gate).astype(BF16)
        return h[0:1, :]

    lax.fori_loop(0, nch, bwd, jnp.zeros((1, W), F32))


def _rglru(xbr, gbr, conv_w, conv_b, wgate, bgate, lam, R):
    B, S, _ = xbr.shape
    W = RG_BLOCK
    return pl.pallas_call(
        functools.partial(_rg_kernel, R=R),
        grid=(B, RG_BLOCKS),
        in_specs=[pl.BlockSpec((1, S, W), lambda b, g: (b, 0, g)),
                  pl.BlockSpec((1, S, W), lambda b, g: (b, 0, g)),
                  pl.BlockSpec((4, W), lambda b, g: (0, g)),
                  pl.BlockSpec((1, W), lambda b, g: (0, g)),
                  pl.BlockSpec((1, W, 4 * W), lambda b, g: (g, 0, 0)),
                  pl.BlockSpec((1, 1, 4 * W), lambda b, g: (g, 0, 0)),
                  pl.BlockSpec((2, W), lambda b, g: (0, g))],
        out_specs=pl.BlockSpec((1, S, W), lambda b, g: (b, 0, g)),
        out_shape=jax.ShapeDtypeStruct((B, S, RG_WIDTH), BF16),
        scratch_shapes=[pltpu.VMEM((S, W), F32), pltpu.VMEM((S, W), F32), pltpu.VMEM((S, W), F32)],
        compiler_params=_cparams(("parallel", "parallel")),
        name="rglru",
    )(xbr, gbr, conv_w, conv_b, wgate, bgate, lam)


def _outproj_kernel(om_ref, og_ref, x_ref, wout_ref, g_ref, h_ref, hn_ref):
    h = x_ref[...] + _dot(om_ref[...], wout_ref[0:MLA_WIDTH, :]) + _dot(og_ref[...], wout_ref[MLA_WIDTH:, :])
    h_ref[...] = h
    hn_ref[...] = _rms(h, g_ref[...]).astype(BF16)


def _outproj(o_mla, o_rg, x, wout, g, tm):
    M, D = x.shape
    return pl.pallas_call(
        _outproj_kernel,
        grid=(M // tm,),
        in_specs=[pl.BlockSpec((tm, MLA_WIDTH), lambda i: (i, 0)),
                  pl.BlockSpec((tm, RG_WIDTH), lambda i: (i, 0)),
                  pl.BlockSpec((tm, D), lambda i: (i, 0)),
                  pl.BlockSpec((D, D), lambda i: (0, 0)),
                  pl.BlockSpec((1, D), lambda i: (0, 0))],
        out_specs=[pl.BlockSpec((tm, D), lambda i: (i, 0)), pl.BlockSpec((tm, D), lambda i: (i, 0))],
        out_shape=[jax.ShapeDtypeStruct((M, D), F32), jax.ShapeDtypeStruct((M, D), BF16)],
        compiler_params=_cparams(("parallel",)),
        name="outproj",
    )(o_mla, o_rg, x, wout, g)


def _memkv_kernel(mem_ref, g_ref, wk_ref, wv_ref, k_ref, v_ref):
    mn = _rms(mem_ref[0], g_ref[...]).astype(BF16)
    k_ref[0] = _dot(mn, wk_ref[...]).astype(BF16)
    v_ref[0] = _dot(mn, wv_ref[...]).astype(BF16)


def _memkv(mem, g, wk, wv):
    B, Mt, D = mem.shape
    return pl.pallas_call(
        _memkv_kernel,
        grid=(B,),
        in_specs=[pl.BlockSpec((1, Mt, D), lambda b: (b, 0, 0)),
                  pl.BlockSpec((1, D), lambda b: (0, 0)),
                  pl.BlockSpec((D, D), lambda b: (0, 0)),
                  pl.BlockSpec((D, D), lambda b: (0, 0))],
        out_specs=[pl.BlockSpec((1, Mt, D), lambda b: (b, 0, 0)), pl.BlockSpec((1, Mt, D), lambda b: (b, 0, 0))],
        out_shape=[jax.ShapeDtypeStruct((B, Mt, D), BF16), jax.ShapeDtypeStruct((B, Mt, D), BF16)],
        compiler_params=_cparams(("parallel",)),
        name="memkv",
    )(mem, g, wk, wv)


def _cross_kernel(hn_ref, h_ref, k_ref, v_ref, wq_ref, wo_ref, g_ref, wr_ref, h2_ref, xn_ref, aff_ref):
    q = _dot(hn_ref[0], wq_ref[...]).astype(BF16)
    outs = []
    for hd in range(X_HEADS):
        c = hd * X_HEAD
        s = _dot_nt(q[:, c:c + X_HEAD], k_ref[0, :, c:c + X_HEAD]) * (X_HEAD ** -0.5)
        e = jnp.exp(s - jnp.max(s, axis=1, keepdims=True))
        p = e / jnp.sum(e, axis=1, keepdims=True)
        outs.append(_dot(p.astype(BF16), v_ref[0, :, c:c + X_HEAD]).astype(BF16))
    h2 = h_ref[0] + _dot(jnp.concatenate(outs, axis=1), wo_ref[...])
    h2_ref[0] = h2
    xn = _rms(h2, g_ref[...])
    _store_slabs(xn_ref, 0, xn.shape[0], xn)
    logits = _dot_nt(wr_ref[...], xn.astype(BF16))
    e = jnp.exp(logits - jnp.max(logits, axis=0, keepdims=True))
    aff_ref[...] = e / jnp.sum(e, axis=0, keepdims=True)


def _cross(hn, h1, kmem, vmem, wq, wo, g, wr_t, tm):
    B, S, D = h1.shape
    nt = S // tm
    return pl.pallas_call(
        _cross_kernel,
        grid=(B, nt),
        in_specs=[pl.BlockSpec((1, tm, D), lambda b, i: (b, i, 0)),
                  pl.BlockSpec((1, tm, D), lambda b, i: (b, i, 0)),
                  pl.BlockSpec((1, MEM_TOKENS, D), lambda b, i: (b, 0, 0)),
                  pl.BlockSpec((1, MEM_TOKENS, D), lambda b, i: (b, 0, 0)),
                  pl.BlockSpec((D, D), lambda b, i: (0, 0)),
                  pl.BlockSpec((D, D), lambda b, i: (0, 0)),
                  pl.BlockSpec((1, D), lambda b, i: (0, 0)),
                  pl.BlockSpec((N_EXPERTS, D), lambda b, i: (0, 0))],
        out_specs=[pl.BlockSpec((1, tm, D), lambda b, i: (b, i, 0)),
                   pl.BlockSpec((tm * SLAB, LANES), lambda b, i: (b * nt + i, 0)),
                   pl.BlockSpec((N_EXPERTS, tm), lambda b, i: (0, b * nt + i))],
        out_shape=[jax.ShapeDtypeStruct((B, S, D), F32),
                   jax.ShapeDtypeStruct((B * S * SLAB, LANES), I32),
                   jax.ShapeDtypeStruct((N_EXPERTS, B * S), F32)],
        compiler_params=_cparams(("parallel", "parallel")),
        name="cross_router",
    )(hn, h1, kmem, vmem, wq, wo, g, wr_t)


NOT_SELECTED = -(1 << 20)


def _select_kernel(aff_ref, idx_ref, pos_ref, *, C, CB):
    nb = aff_ref.shape[1]
    L = LANES
    li = lax.broadcasted_iota(I32, (L, L), 0)
    lj = lax.broadcasted_iota(I32, (L, L), 1)
    upper_l = (li <= lj).astype(BF16)
    ones_l = jnp.ones((L, L), BF16)
    bi = lax.broadcasted_iota(I32, (nb, nb), 0)
    bj = lax.broadcasted_iota(I32, (nb, nb), 1)
    lower_strict = (bj < bi).astype(BF16)
    upper_b = (bi <= bj).astype(BF16)
    blk_id = lax.broadcasted_iota(I32, (nb, L), 0).astype(F32)
    lane = lax.broadcasted_iota(I32, (CB, L), 1)

    idx_ref[...] = jnp.zeros_like(idx_ref)

    def count(m):
        return jnp.sum(jnp.sum(m.astype(I32), axis=1, keepdims=True), axis=0, keepdims=True)

    def prefix(m_b):
        within = _dot(m_b, upper_l)
        tot = _dot(m_b, ones_l)
        t_excl = _dot(lower_strict, tot.astype(BF16))
        return within, tot, t_excl

    def per_expert(e, _):
        a = aff_ref[e]
        bits = pltpu.bitcast(a, I32)

        def bisect(i, thr):
            cand = thr | jnp.left_shift(jnp.int32(1), 30 - i)
            return jnp.where(count(bits >= cand) >= C, cand, thr)

        thr = lax.fori_loop(0, 31, bisect, jnp.zeros((1, 1), I32))
        gt = bits > thr
        eq = bits == thr
        need = (C - count(gt)).astype(F32)
        e_within, _, e_excl = prefix(eq.astype(BF16))
        eq_rank = e_within - eq.astype(F32) + e_excl
        mask = gt | (eq & (eq_rank < need))
        mask_f = mask.astype(F32)
        within, tot, t_excl = prefix(mask.astype(BF16))
        pos = (within - mask_f + t_excl).astype(I32)
        pos_ref[e] = jnp.where(mask, pos, pos + NOT_SELECTED)

        tot_row = _dot_nt(ones_l, mask.astype(BF16))
        t_incl_row = _dot(tot_row.astype(BF16), upper_b)[0:1, :]
        t_excl_row = t_incl_row - tot_row[0:1, :]
        hi = jnp.floor(t_excl * (1.0 / 256.0))
        rhs = jnp.concatenate([within, hi, t_excl - 256.0 * hi, blk_id], axis=1).astype(BF16)

        def per_chunk(ci, _):
            c0 = pl.multiple_of(ci * CB, CB)
            c_b = (c0 + lax.broadcasted_iota(I32, (CB, nb), 0)).astype(F32)
            onehot = ((t_excl_row <= c_b) & (c_b < t_incl_row)).astype(BF16)
            g = _dot(onehot, rhs)
            c_l = (c0 + lax.broadcasted_iota(I32, (CB, L), 0)).astype(F32)
            inside = (g[:, 0:L] + (256.0 * g[:, L:2 * L] + g[:, 2 * L:3 * L])) <= c_l
            val = (g[:, 3 * L:4 * L] * float(L) + _dot(inside.astype(BF16), ones_l)).astype(I32)
            idx_ref[pl.ds(c0, CB), :] = jnp.where(lane == e, val, idx_ref[pl.ds(c0, CB), :])
            return 0

        lax.fori_loop(0, C // CB, per_chunk, 0)
        return 0

    lax.fori_loop(0, N_EXPERTS, per_expert, 0)


def _select(aff3, C):
    E, nb, L = aff3.shape
    CB = min(C, 512)
    return pl.pallas_call(
        functools.partial(_select_kernel, C=C, CB=CB),
        out_shape=[jax.ShapeDtypeStruct((C, L), I32),
                   jax.ShapeDtypeStruct((E, nb, L), I32)],
        compiler_params=pltpu.CompilerParams(vmem_limit_bytes=VMEM_LIMIT),
        name="ec_select",
    )(aff3)


def _ffn_kernel(idx_ref, x_hbm, wg_ref, wu_ref, wd_ref, o_ref, xbuf, sem, *, tc, nsteps):
    step = pl.program_id(0) * pl.num_programs(1) + pl.program_id(1)
    slot = lax.rem(step, 2)

    def fetch(s, dst_slot):
        for u in range(tc):
            tok = idx_ref[s * tc + u]
            pltpu.make_async_copy(x_hbm.at[pl.ds(tok * SLAB, SLAB)],
                                  xbuf.at[pl.ds((dst_slot * tc + u) * SLAB, SLAB)], sem.at[dst_slot]).start()

    def wait(dst_slot):
        pltpu.make_async_copy(x_hbm.at[pl.ds(0, tc * SLAB)],
                              xbuf.at[pl.ds(dst_slot * tc * SLAB, tc * SLAB)], sem.at[dst_slot]).wait()

    @pl.when(step == 0)
    def _():
        fetch(0, 0)

    fetch(jnp.minimum(step + 1, nsteps - 1), 1 - slot)
    wait(slot)

    x = _load_slabs(xbuf, slot * tc, tc)
    hid = jax.nn.silu(_dot(x, wg_ref[0])) * _dot(x, wu_ref[0])
    _store_slabs(o_ref, 0, tc, _dot(hid.astype(BF16), wd_ref[0]))

    @pl.when(step == nsteps - 1)
    def _():
        wait(1 - slot)


def _ffn(idx, xn_slabs, wg, wu, wd, C, tc):
    E, D, F = wg.shape
    nt = C // tc
    return pl.pallas_call(
        functools.partial(_ffn_kernel, tc=tc, nsteps=E * nt),
        grid_spec=pltpu.PrefetchScalarGridSpec(
            num_scalar_prefetch=1,
            grid=(E, nt),
            in_specs=[pl.BlockSpec(memory_space=pl.ANY),
                      pl.BlockSpec((1, D, F), lambda e, i, idx: (e, 0, 0)),
                      pl.BlockSpec((1, D, F), lambda e, i, idx: (e, 0, 0)),
                      pl.BlockSpec((1, F, D), lambda e, i, idx: (e, 0, 0))],
            out_specs=pl.BlockSpec((tc * SLAB, LANES), lambda e, i, idx: (e * nt + i, 0)),
            scratch_shapes=[pltpu.VMEM((2 * tc * SLAB, LANES), I32), pltpu.SemaphoreType.DMA((2,))]),
        out_shape=jax.ShapeDtypeStruct((E * C * SLAB, LANES), I32),
        compiler_params=_cparams(("arbitrary", "arbitrary")),
        name="ec_ffn",
    )(idx, xn_slabs, wg, wu, wd)


COMBINE_W = 64


def _split_bf16(g):
    hi = g.astype(BF16)
    return hi, (g - hi.astype(F32)).astype(BF16)


def _combine_kernel(start_ref, pos_ref, w_ref, h_ref, ye_hbm, g_ref, o_ref, ybuf, ybuf2, acc_ref, sem,
                    *, tm, C, nt):
    E, W = N_EXPERTS, COMBINE_W
    i = pl.program_id(0)
    slot = lax.rem(i, 2)

    def first_slot(tile, e):
        return jnp.minimum(start_ref[tile * E + e], C - W)

    def fetch(tile, dst_slot):
        for e in range(E):
            pltpu.make_async_copy(ye_hbm.at[pl.ds((e * C + first_slot(tile, e)) * SLAB, W * SLAB)],
                                  ybuf.at[pl.ds((dst_slot * E + e) * W * SLAB, W * SLAB)], sem.at[dst_slot]).start()

    def wait(dst_slot):
        pltpu.make_async_copy(ye_hbm.at[pl.ds(0, E * W * SLAB)],
                              ybuf.at[pl.ds(dst_slot * E * W * SLAB, E * W * SLAB)], sem.at[dst_slot]).wait()

    @pl.when(i == 0)
    def _():
        fetch(0, 0)

    fetch(jnp.minimum(i + 1, nt - 1), 1 - slot)
    wait(slot)

    pos = pos_ref[...]
    w = w_ref[...]
    lane = lax.broadcasted_iota(I32, (tm, 2 * W), 1)
    pieces = []
    for p in range(E // 2):
        e0, e1 = 2 * p, 2 * p + 1
        k0 = pos[:, e0:e0 + 1] - first_slot(i, e0)
        k1 = pos[:, e1:e1 + 1] - first_slot(i, e1) + W
        pieces.append(jnp.where((lane == k0) & (lane < W), w[:, e0:e0 + 1],
                                jnp.where((lane == k1) & (lane >= W), w[:, e1:e1 + 1], 0.0)))
    g_hi, g_lo = _split_bf16(jnp.concatenate(pieces, axis=1))
    y = _load_slabs(ybuf, slot * E * W, E * W)
    acc_ref[...] = h_ref[...] + _dot(g_hi, y) + _dot(g_lo, y)

    lane_w = lax.broadcasted_iota(I32, (tm, W), 1)
    for e in range(E):
        s0 = first_slot(i, e)
        n_extra = jnp.maximum((start_ref[(i + 1) * E + e] - s0 + W - 1) // W - 1, 0)

        def extra(kk, _, e=e, s0=s0):
            lo_slot = s0 + (kk + 1) * W
            s2 = jnp.minimum(lo_slot, C - W)
            cp = pltpu.make_async_copy(ye_hbm.at[pl.ds((e * C + s2) * SLAB, W * SLAB)], ybuf2, sem.at[2])
            cp.start()
            cp.wait()
            pe = pos[:, e:e + 1]
            gx_hi, gx_lo = _split_bf16(jnp.where((lane_w == pe - s2) & (pe >= lo_slot), w[:, e:e + 1], 0.0))
            y2 = _load_slabs(ybuf2, 0, W)
            acc_ref[...] += _dot(gx_hi, y2) + _dot(gx_lo, y2)
            return 0

        lax.fori_loop(0, n_extra, extra, 0)

    o_ref[...] = _rms(acc_ref[...], g_ref[...])

    @pl.when(i == nt - 1)
    def _():
        wait(1 - slot)


def _combine(start, pos_t, w_t, h2, ye_slabs, g, C, tm):
    n, D = h2.shape
    nt = n // tm
    E, W = N_EXPERTS, COMBINE_W
    assert C >= W
    return pl.pallas_call(
        functools.partial(_combine_kernel, tm=tm, C=C, nt=nt),
        grid_spec=pltpu.PrefetchScalarGridSpec(
            num_scalar_prefetch=1,
            grid=(nt,),
            in_specs=[pl.BlockSpec((tm, E), lambda i, st: (i, 0)),
                      pl.BlockSpec((tm, E), lambda i, st: (i, 0)),
                      pl.BlockSpec((tm, D), lambda i, st: (i, 0)),
                      pl.BlockSpec(memory_space=pl.ANY),
                      pl.BlockSpec((1, D), lambda i, st: (0, 0))],
            out_specs=pl.BlockSpec((tm, D), lambda i, st: (i, 0)),
            scratch_shapes=[pltpu.VMEM((2 * E * W * SLAB, LANES), I32),
                            pltpu.VMEM((W * SLAB, LANES), I32),
                            pltpu.VMEM((tm, D), F32),
                            pltpu.SemaphoreType.DMA((3,))]),
        out_shape=jax.ShapeDtypeStruct((n, D), F32),
        compiler_params=_cparams(("arbitrary",)),
        name="ec_combine",
    )(start, pos_t, w_t, h2, ye_slabs, g)


def _pick(n, prefs):
    for p in prefs:
        if n % p == 0:
            return p
    raise ValueError(f"no tile in {prefs} divides {n}")


def _prep_weights(norm_mix, w_in, q_norm, w_uq, kv_norm, w_ukv, conv_w, conv_b, w_rg_a, b_rg_a, w_rg_x, b_rg_x,
                  rg_lambda, w_out, norm_cross, norm_mem, w_cq, w_ck, w_cv, w_co, norm_ffn, w_router,
                  w_gate, w_up, w_down, norm_final):
    swap = (jnp.arange(QK_ROPE) + QK_ROPE // 2) % QK_ROPE
    o = Q_LORA + KV_LORA
    k_r = w_in[:, o:o + QK_ROPE]
    win = jnp.concatenate([w_in[:, :o], w_in[:, o + QK_ROPE:], k_r, k_r[:, swap]], axis=1).astype(BF16)
    uq = w_uq.reshape(Q_LORA, MLA_HEADS, QK_NOPE + QK_ROPE)
    rope = uq[:, :, QK_NOPE:]
    wuq = jnp.concatenate([uq[:, :, :QK_NOPE], rope, rope[:, :, swap]], axis=2).reshape(Q_LORA, MLA_HEADS * HEAD_QK)
    wgate = jnp.concatenate([w_rg_a[0], w_rg_a[1], w_rg_x[0], w_rg_x[1]], axis=2).astype(BF16)
    bgate = jnp.concatenate([b_rg_a[0], b_rg_a[1], b_rg_x[0], b_rg_x[1]], axis=1)[:, None, :]
    row = lambda v: v.reshape(1, -1)
    return dict(
        gmix=row(norm_mix), win=win, qg=row(q_norm), wuq=wuq.astype(BF16), kvg=row(kv_norm), wukv=w_ukv.astype(BF16),
        conv_w=conv_w, conv_b=row(conv_b), wgate=wgate, bgate=bgate, lam=rg_lambda,
        wout=w_out.astype(BF16), gcross=row(norm_cross), gmem=row(norm_mem),
        wcq=w_cq.astype(BF16), wck=w_ck.astype(BF16), wcv=w_cv.astype(BF16), wco=w_co.astype(BF16),
        gffn=row(norm_ffn), wr_t=w_router.T.astype(BF16),
        wg=w_gate.astype(BF16), wu=w_up.astype(BF16), wd=w_down.astype(BF16), gfinal=row(norm_final))


def _rope_mults(S):
    inv = ROPE_THETA ** (-jnp.arange(0, QK_ROPE, 2, dtype=F32) / QK_ROPE)
    ang = jnp.arange(S, dtype=F32)[:, None] * inv[None, :]
    cos, sin = jnp.cos(ang), jnp.sin(ang)
    kmult = jnp.concatenate([cos, cos, -sin, sin], axis=1)
    scale = (QK_NOPE + QK_ROPE) ** -0.5 * 1.4426950408889634
    qmult = scale * jnp.concatenate([jnp.ones((S, QK_NOPE), F32), kmult], axis=1)
    return qmult, kmult


def _run(x, mem, W):
    B, S, D = x.shape
    n = B * S
    C = EC_CAPACITY_FACTOR * n // N_EXPERTS
    qmult, kmult = _rope_mults(S)

    q, k, v, xbr, gbr = _inproj(x, W["gmix"], W["win"], W["qg"], W["wuq"], W["kvg"], W["wukv"], qmult, kmult,
                                tm=_pick(S, (256, 128)))
    o_mla = _attention(q, k, v, tq=_pick(S, (512, 256, 128)), tk=_pick(S, (512, 256, 128)))
    o_rg = _rglru(xbr, gbr, W["conv_w"], W["conv_b"], W["wgate"], W["bgate"], W["lam"], R=_pick(S, (256, 128)))
    h1, hn = _outproj(o_mla.reshape(n, MLA_WIDTH), o_rg.reshape(n, RG_WIDTH), x.reshape(n, D), W["wout"],
                      W["gcross"], tm=_pick(n, (512, 256, 128)))
    kmem, vmem = _memkv(mem, W["gmem"], W["wck"], W["wcv"])
    h2, xn, aff_t = _cross(hn.reshape(B, S, D), h1.reshape(B, S, D), kmem, vmem, W["wcq"], W["wco"], W["gffn"],
                           W["wr_t"], tm=_pick(S, (256, 128)))

    nb = max(n // LANES, LANES)
    aff3 = jnp.pad(aff_t, ((0, 0), (0, nb * LANES - n)), constant_values=-1.0).reshape(N_EXPERTS, nb, LANES)
    idx_t, posm = _select(aff3, C)
    idx = idx_t[:, :N_EXPERTS].T.reshape(N_EXPERTS * C)
    ye = _ffn(idx, xn, W["wg"], W["wu"], W["wd"], C, tc=_pick(C, (256, 128)))

    tm = _pick(n, (256, 128))
    posm = posm.reshape(N_EXPERTS, nb * LANES)[:, :n]
    first = posm[:, ::tm]
    first = jnp.where(first < 0, first - NOT_SELECTED, first)
    start = jnp.concatenate([first.T, jnp.full((1, N_EXPERTS), C, I32)], axis=0).reshape(-1)
    y = _combine(start, posm.T, aff_t.T, h2.reshape(n, D), ye, W["gfinal"], C, tm=tm)
    return y.reshape(B, S, D)


def kernel(x_prompt, x_sample, mem_prompt, mem_sample, norm_mix, w_in, q_norm, w_uq, kv_norm, w_ukv, conv_w, conv_b,
           w_rg_a, b_rg_a, w_rg_x, b_rg_x, rg_lambda, w_out, norm_cross, norm_mem, w_cq, w_ck, w_cv, w_co, norm_ffn,
           w_router, w_gate, w_up, w_down, norm_final):
    W = _prep_weights(norm_mix[0], w_in[0], q_norm[0], w_uq[0], kv_norm[0], w_ukv[0], conv_w[0], conv_b[0],
                      w_rg_a[0], b_rg_a[0], w_rg_x[0], b_rg_x[0], rg_lambda[0], w_out[0], norm_cross[0],
                      norm_mem[0], w_cq[0], w_ck[0], w_cv[0], w_co[0], norm_ffn[0], w_router[0],
                      w_gate[0], w_up[0], w_down[0], norm_final)
    return (_run(x_prompt, mem_prompt, W), _run(x_sample, mem_sample, W))
```

```python
import functools

import jax
import jax.numpy as jnp
from jax import lax
from jax.experimental import pallas as pl
from jax.experimental.pallas import tpu as pltpu

F32 = jnp.float32
BF16 = jnp.bfloat16
I32 = jnp.int32

D_MODEL = 2048
MLA_HEADS = 8
QK_NOPE = 128
QK_ROPE = 64
V_HEAD = 128
Q_LORA = 512
KV_LORA = 512
MLA_WIDTH = MLA_HEADS * V_HEAD
RG_WIDTH = D_MODEL - MLA_WIDTH
RG_BLOCKS = 8
RG_BLOCK = RG_WIDTH // RG_BLOCKS
LRU_C = 8.0
MEM_TOKENS = 256
X_HEADS = 4
X_HEAD = D_MODEL // X_HEADS
N_EXPERTS = 16
EC_CAPACITY_FACTOR = 2
EXPERT_FF = 1408
ROPE_THETA = 10000.0
EPS = 1e-6

LANES = 128
HEAD_QK = 2 * LANES
VMEM_LIMIT = 56 * 1024 * 1024


def _cparams(sem, vmem=VMEM_LIMIT):
    return pltpu.CompilerParams(dimension_semantics=sem, vmem_limit_bytes=vmem)


def _rms(x, g):
    ms = jnp.mean(x * x, axis=-1, keepdims=True)
    return x * lax.rsqrt(ms + EPS) * g


def _sigmoid(x):
    return 0.5 * jnp.tanh(0.5 * x) + 0.5


def _dot(a, b):
    return jnp.dot(a, b, preferred_element_type=F32)


def _dot_nt(a, b):
    return lax.dot_general(a, b, (((1,), (1,)), ((), ())), preferred_element_type=F32)


SLAB = D_MODEL // (2 * LANES)
HI_MASK = -65536


def _store_slabs(ref, row0, m, y):
    for j in range(SLAB):
        lo = pltpu.bitcast(y[:, j * LANES:(j + 1) * LANES].astype(BF16).astype(F32), I32)
        hi = pltpu.bitcast(y[:, (j + SLAB) * LANES:(j + SLAB + 1) * LANES].astype(BF16).astype(F32), I32)
        ref[pl.ds(row0 * SLAB + j, m, stride=SLAB), :] = (hi & HI_MASK) | lax.shift_right_logical(lo, 16)


def _load_slabs(ref, row0, m):
    lo, hi = [], []
    for j in range(SLAB):
        w = ref[pl.ds(row0 * SLAB + j, m, stride=SLAB), :]
        lo.append(pltpu.bitcast(lax.shift_left(w, 16), F32).astype(BF16))
        hi.append(pltpu.bitcast(w & HI_MASK, F32).astype(BF16))
    return jnp.concatenate(lo + hi, axis=1)


def _inproj_kernel(x_ref, gmix_ref, win_ref, qg_ref, wuq_ref, kvg_ref, wukv_ref,
                   qmult_ref, kmult_ref, q_ref, k_ref, v_ref, xbr_ref, gbr_ref):
    hn = _rms(x_ref[0], gmix_ref[...]).astype(BF16)
    cq = _dot(hn, win_ref[:, 0:Q_LORA])
    ckv = _dot(hn, win_ref[:, Q_LORA:Q_LORA + KV_LORA])
    o = Q_LORA + KV_LORA
    xbr_ref[0] = _dot(hn, win_ref[:, o:o + RG_WIDTH])
    gbr_ref[0] = _dot(hn, win_ref[:, o + RG_WIDTH:o + 2 * RG_WIDTH])
    kr = _dot(hn, win_ref[:, o + 2 * RG_WIDTH:o + 2 * RG_WIDTH + LANES])
    tk = kr * kmult_ref[...]
    rkk = (tk + pltpu.roll(tk, LANES // 2, 1)).astype(BF16)

    q = _dot(_rms(cq, qg_ref[...]).astype(BF16), wuq_ref[...])
    kv = _dot(_rms(ckv, kvg_ref[...]).astype(BF16), wukv_ref[...])
    qm = qmult_ref[...]
    for h in range(MLA_HEADS):
        c = h * HEAD_QK
        q_ref[0, h] = (q[:, c:c + HEAD_QK] * qm).astype(BF16)
        k_ref[0, h, :, 0:LANES] = kv[:, c:c + LANES].astype(BF16)
        k_ref[0, h, :, LANES:HEAD_QK] = rkk
        v_ref[0, h, :, 0:LANES] = kv[:, c + LANES:c + HEAD_QK].astype(BF16)
        v_ref[0, h, :, LANES:HEAD_QK] = jnp.ones((kv.shape[0], LANES), BF16)


def _inproj(x, gmix, win, qg, wuq, kvg, wukv, qmult, kmult, tm):
    B, S, D = x.shape
    full = lambda a: pl.BlockSpec(a.shape, lambda b, i: (0,) * a.ndim)
    return pl.pallas_call(
        _inproj_kernel,
        grid=(B, S // tm),
        in_specs=[pl.BlockSpec((1, tm, D), lambda b, i: (b, i, 0)),
                  full(gmix), full(win), full(qg), full(wuq), full(kvg), full(wukv),
                  pl.BlockSpec((tm, HEAD_QK), lambda b, i: (i, 0)),
                  pl.BlockSpec((tm, LANES), lambda b, i: (i, 0))],
        out_specs=[pl.BlockSpec((1, MLA_HEADS, tm, HEAD_QK), lambda b, i: (b, 0, i, 0)),
                   pl.BlockSpec((1, MLA_HEADS, tm, HEAD_QK), lambda b, i: (b, 0, i, 0)),
                   pl.BlockSpec((1, MLA_HEADS, tm, 2 * V_HEAD), lambda b, i: (b, 0, i, 0)),
                   pl.BlockSpec((1, tm, RG_WIDTH), lambda b, i: (b, i, 0)),
                   pl.BlockSpec((1, tm, RG_WIDTH), lambda b, i: (b, i, 0))],
        out_shape=[jax.ShapeDtypeStruct((B, MLA_HEADS, S, HEAD_QK), BF16),
                   jax.ShapeDtypeStruct((B, MLA_HEADS, S, HEAD_QK), BF16),
                   jax.ShapeDtypeStruct((B, MLA_HEADS, S, 2 * V_HEAD), BF16),
                   jax.ShapeDtypeStruct((B, S, RG_WIDTH), F32),
                   jax.ShapeDtypeStruct((B, S, RG_WIDTH), F32)],
        compiler_params=_cparams(("parallel", "parallel")),
        name="inproj",
    )(x, gmix, win, qg, wuq, kvg, wukv, qmult, kmult)


def _attn_kernel(q_ref, k_ref, v_ref, o_ref, *, tk):
    q = q_ref[0, 0]
    tq = q.shape[0]
    nk = k_ref.shape[2] // tk

    def body(j, carry):
        m, acc = carry
        off = pl.multiple_of(j * tk, tk)
        s = _dot_nt(q, k_ref[0, 0, pl.ds(off, tk), :])
        m_new = jnp.maximum(m, jnp.max(s, axis=1, keepdims=True))
        alpha = jnp.exp2(m - m_new)
        p = jnp.exp2(s - m_new)
        acc = alpha * acc + _dot(p.astype(BF16), v_ref[0, 0, pl.ds(off, tk), :])
        return m_new, acc

    init = (jnp.full((tq, 1), -jnp.inf, F32), jnp.zeros((tq, 2 * V_HEAD), F32))
    _, acc = lax.fori_loop(0, nk, body, init, unroll=8)
    o_ref[0] = (acc[:, :V_HEAD] / acc[:, V_HEAD:]).astype(BF16)


def _attention(q, k, v, tq, tk):
    B, H, S, _ = q.shape
    return pl.pallas_call(
        functools.partial(_attn_kernel, tk=tk),
        grid=(B, H, S // tq),
        in_specs=[pl.BlockSpec((1, 1, tq, HEAD_QK), lambda b, h, i: (b, h, i, 0)),
                  pl.BlockSpec((1, 1, S, HEAD_QK), lambda b, h, i: (b, h, 0, 0)),
                  pl.BlockSpec((1, 1, S, 2 * V_HEAD), lambda b, h, i: (b, h, 0, 0))],
        out_specs=pl.BlockSpec((1, tq, V_HEAD), lambda b, h, i: (b, i, h)),
        out_shape=jax.ShapeDtypeStruct((B, S, H * V_HEAD), BF16),
        compiler_params=_cparams(("parallel", "parallel", "parallel")),
        name="mla_attention",
    )(q, k, v)


def _rg_kernel(x_ref, g_ref, cw_ref, cb_ref, wg_ref, bg_ref, lam_ref, o_ref,
               hf_ref, ab_ref, ub_ref, *, R):
    S = x_ref.shape[1]
    nch = S // R
    W = RG_BLOCK
    cw = cw_ref[...]
    cb = cb_ref[...]
    ls = LRU_C * jax.nn.log_sigmoid(lam_ref[...])
    rows = lax.broadcasted_iota(I32, (R, W), 0)
    next_rows = R + 16

    def gates(c):
        r0 = pl.multiple_of(c * R, R)
        cur = x_ref[0, pl.ds(r0, R), :]
        prev8 = x_ref[0, pl.ds(pl.multiple_of(jnp.maximum(r0 - 8, 0), 8), 8), :]
        prev8 = jnp.where(c > 0, prev8, 0.0)
        next8 = x_ref[0, pl.ds(pl.multiple_of(jnp.minimum(r0 + R, S - 8), 8), 8), :]
        next8 = jnp.where(c < nch - 1, next8, 0.0)
        ext = jnp.concatenate([prev8, cur, next8], axis=0)
        xm1 = pltpu.roll(ext, 1, 0)[8:8 + R]
        xp1 = pltpu.roll(ext, next_rows - 1, 0)[8:8 + R]
        xp2 = pltpu.roll(ext, next_rows - 2, 0)[8:8 + R]
        xc = cb + xm1 * cw[0:1] + cur * cw[1:2] + xp1 * cw[2:3] + xp2 * cw[3:4]
        sg = _sigmoid(_dot(xc.astype(BF16), wg_ref[0]) + bg_ref[0])
        out = []
        for n in range(2):
            log_a = sg[:, n * W:(n + 1) * W] * ls[n:n + 1]
            a = jnp.exp(log_a)
            t = jnp.tanh(log_a)
            p = -2.0 * t
            root = jnp.where(p > 0.0, p * lax.rsqrt(p * (1.0 - t)), 0.0)
            u = root * sg[:, (2 + n) * W:(3 + n) * W] * xc
            out.append((a, u))
        return out

    def scan(a, u, reverse):
        d = 1
        while d < 8:
            if reverse:
                keep = rows < R - d
                sh = R - d
            else:
                keep = rows >= d
                sh = d
            a_s = jnp.where(keep, pltpu.roll(a, sh, 0), 1.0)
            u_s = jnp.where(keep, pltpu.roll(u, sh, 0), 0.0)
            u = a * u_s + u
            a = a * a_s
            d *= 2
        while d < R:
            if reverse:
                u = jnp.concatenate([a[:R - d] * u[d:] + u[:R - d], u[R - d:]], axis=0)
                a = jnp.concatenate([a[:R - d] * a[d:], a[R - d:]], axis=0)
            else:
                u = jnp.concatenate([u[:d], a[d:] * u[:R - d] + u[d:]], axis=0)
                a = jnp.concatenate([a[:d], a[d:] * a[:R - d]], axis=0)
            d *= 2
        return a, u

    def fwd(c, carry):
        r0 = pl.multiple_of(c * R, R)
        (a_f, u_f), (a_b, u_b) = gates(c)
        ab_ref[pl.ds(r0, R), :] = a_b
        ub_ref[pl.ds(r0, R), :] = u_b
        acum, hloc = scan(a_f, u_f, False)
        h = acum * carry + hloc
        hf_ref[pl.ds(r0, R), :] = h
        return h[R - 1:R, :]

    lax.fori_loop(0, nch, fwd, jnp.zeros((1, W), F32))

    def bwd(i, carry):
        c = nch - 1 - i
        r0 = pl.multiple_of(c * R, R)
        acum, hloc = scan(ab_ref[pl.ds(r0, R), :], ub_ref[pl.ds(r0, R), :], True)
        h = acum * carry + hloc
        gate = jax.nn.gelu(g_ref[0, pl.ds(r0, R), :], approximate=True)
        o_ref[0, pl.ds(r0, R), :] = ((hf_ref[pl.ds(r0, R), :] + h) * gate).astype(BF16)
        return h[0:1, :]

    lax.fori_loop(0, nch, bwd, jnp.zeros((1, W), F32))


def _rglru(xbr, gbr, conv_w, conv_b, wgate, bgate, lam, R):
    B, S, _ = xbr.shape
    W = RG_BLOCK
    return pl.pallas_call(
        functools.partial(_rg_kernel, R=R),
        grid=(B, RG_BLOCKS),
        in_specs=[pl.BlockSpec((1, S, W), lambda b, g: (b, 0, g)),
                  pl.BlockSpec((1, S, W), lambda b, g: (b, 0, g)),
                  pl.BlockSpec((4, W), lambda b, g: (0, g)),
                  pl.BlockSpec((1, W), lambda b, g: (0, g)),
                  pl.BlockSpec((1, W, 4 * W), lambda b, g: (g, 0, 0)),
                  pl.BlockSpec((1, 1, 4 * W), lambda b, g: (g, 0, 0)),
                  pl.BlockSpec((2, W), lambda b, g: (0, g))],
        out_specs=pl.BlockSpec((1, S, W), lambda b, g: (b, 0, g)),
        out_shape=jax.ShapeDtypeStruct((B, S, RG_WIDTH), BF16),
        scratch_shapes=[pltpu.VMEM((S, W), F32), pltpu.VMEM((S, W), F32), pltpu.VMEM((S, W), F32)],
        compiler_params=_cparams(("parallel", "parallel")),
        name="rglru",
    )(xbr, gbr, conv_w, conv_b, wgate, bgate, lam)


def _outproj_kernel(om_ref, og_ref, x_ref, wout_ref, g_ref, h_ref, hn_ref):
    h = x_ref[...] + _dot(om_ref[...], wout_ref[0:MLA_WIDTH, :]) + _dot(og_ref[...], wout_ref[MLA_WIDTH:, :])
    h_ref[...] = h
    hn_ref[...] = _rms(h, g_ref[...]).astype(BF16)


def _outproj(o_mla, o_rg, x, wout, g, tm):
    M, D = x.shape
    return pl.pallas_call(
        _outproj_kernel,
        grid=(M // tm,),
        in_specs=[pl.BlockSpec((tm, MLA_WIDTH), lambda i: (i, 0)),
                  pl.BlockSpec((tm, RG_WIDTH), lambda i: (i, 0)),
                  pl.BlockSpec((tm, D), lambda i: (i, 0)),
                  pl.BlockSpec((D, D), lambda i: (0, 0)),
                  pl.BlockSpec((1, D), lambda i: (0, 0))],
        out_specs=[pl.BlockSpec((tm, D), lambda i: (i, 0)), pl.BlockSpec((tm, D), lambda i: (i, 0))],
        out_shape=[jax.ShapeDtypeStruct((M, D), F32), jax.ShapeDtypeStruct((M, D), BF16)],
        compiler_params=_cparams(("parallel",)),
        name="outproj",
    )(o_mla, o_rg, x, wout, g)


def _memkv_kernel(mem_ref, g_ref, wk_ref, wv_ref, k_ref, v_ref):
    mn = _rms(mem_ref[0], g_ref[...]).astype(BF16)
    k_ref[0] = _dot(mn, wk_ref[...]).astype(BF16)
    v_ref[0] = _dot(mn, wv_ref[...]).astype(BF16)


def _memkv(mem, g, wk, wv):
    B, Mt, D = mem.shape
    return pl.pallas_call(
        _memkv_kernel,
        grid=(B,),
        in_specs=[pl.BlockSpec((1, Mt, D), lambda b: (b, 0, 0)),
                  pl.BlockSpec((1, D), lambda b: (0, 0)),
                  pl.BlockSpec((D, D), lambda b: (0, 0)),
                  pl.BlockSpec((D, D), lambda b: (0, 0))],
        out_specs=[pl.BlockSpec((1, Mt, D), lambda b: (b, 0, 0)), pl.BlockSpec((1, Mt, D), lambda b: (b, 0, 0))],
        out_shape=[jax.ShapeDtypeStruct((B, Mt, D), BF16), jax.ShapeDtypeStruct((B, Mt, D), BF16)],
        compiler_params=_cparams(("parallel",)),
        name="memkv",
    )(mem, g, wk, wv)


def _cross_kernel(hn_ref, h_ref, k_ref, v_ref, wq_ref, wo_ref, g_ref, wr_ref, h2_ref, xn_ref, aff_ref):
    q = _dot(hn_ref[0], wq_ref[...]).astype(BF16)
    outs = []
    for hd in range(X_HEADS):
        c = hd * X_HEAD
        s = _dot_nt(q[:, c:c + X_HEAD], k_ref[0, :, c:c + X_HEAD]) * (X_HEAD ** -0.5)
        e = jnp.exp(s - jnp.max(s, axis=1, keepdims=True))
        p = e * (1.0 / jnp.sum(e, axis=1, keepdims=True))
        outs.append(_dot(p.astype(BF16), v_ref[0, :, c:c + X_HEAD]).astype(BF16))
    h2 = h_ref[0] + _dot(jnp.concatenate(outs, axis=1), wo_ref[...])
    h2_ref[0] = h2
    xn = _rms(h2, g_ref[...])
    _store_slabs(xn_ref, 0, xn.shape[0], xn)
    logits = _dot_nt(wr_ref[...], xn.astype(BF16))
    e = jnp.exp(logits - jnp.max(logits, axis=0, keepdims=True))
    aff_ref[...] = e / jnp.sum(e, axis=0, keepdims=True)


def _cross(hn, h1, kmem, vmem, wq, wo, g, wr_t, tm):
    B, S, D = h1.shape
    nt = S // tm
    return pl.pallas_call(
        _cross_kernel,
        grid=(B, nt),
        in_specs=[pl.BlockSpec((1, tm, D), lambda b, i: (b, i, 0)),
                  pl.BlockSpec((1, tm, D), lambda b, i: (b, i, 0)),
                  pl.BlockSpec((1, MEM_TOKENS, D), lambda b, i: (b, 0, 0)),
                  pl.BlockSpec((1, MEM_TOKENS, D), lambda b, i: (b, 0, 0)),
                  pl.BlockSpec((D, D), lambda b, i: (0, 0)),
                  pl.BlockSpec((D, D), lambda b, i: (0, 0)),
                  pl.BlockSpec((1, D), lambda b, i: (0, 0)),
                  pl.BlockSpec((N_EXPERTS, D), lambda b, i: (0, 0))],
        out_specs=[pl.BlockSpec((1, tm, D), lambda b, i: (b, i, 0)),
                   pl.BlockSpec((tm * SLAB, LANES), lambda b, i: (b * nt + i, 0)),
                   pl.BlockSpec((N_EXPERTS, tm), lambda b, i: (0, b * nt + i))],
        out_shape=[jax.ShapeDtypeStruct((B, S, D), F32),
                   jax.ShapeDtypeStruct((B * S * SLAB, LANES), I32),
                   jax.ShapeDtypeStruct((N_EXPERTS, B * S), F32)],
        compiler_params=_cparams(("parallel", "parallel")),
        name="cross_router",
    )(hn, h1, kmem, vmem, wq, wo, g, wr_t)


NOT_SELECTED = -(1 << 20)


def _select_kernel(aff_ref, idx_ref, pos_ref, *, C, CB):
    nb = aff_ref.shape[1]
    L = LANES
    li = lax.broadcasted_iota(I32, (L, L), 0)
    lj = lax.broadcasted_iota(I32, (L, L), 1)
    upper_l = (li <= lj).astype(BF16)
    ones_l = jnp.ones((L, L), BF16)
    bi = lax.broadcasted_iota(I32, (nb, nb), 0)
    bj = lax.broadcasted_iota(I32, (nb, nb), 1)
    lower_strict = (bj < bi).astype(BF16)
    upper_b = (bi <= bj).astype(BF16)
    blk_id = lax.broadcasted_iota(I32, (nb, L), 0).astype(F32)
    lane = lax.broadcasted_iota(I32, (CB, L), 1)

    idx_ref[...] = jnp.zeros_like(idx_ref)

    def count(m):
        return jnp.sum(jnp.sum(m.astype(I32), axis=1, keepdims=True), axis=0, keepdims=True)

    def prefix(m_b):
        within = _dot(m_b, upper_l)
        tot = _dot(m_b, ones_l)
        t_excl = _dot(lower_strict, tot.astype(BF16))
        return within, tot, t_excl

    def per_expert(e, _):
        a = aff_ref[e]
        bits = pltpu.bitcast(a, I32)

        def bisect(i, thr):
            cand = thr | jnp.left_shift(jnp.int32(1), 30 - i)
            return jnp.where(count(bits >= cand) >= C, cand, thr)

        thr = lax.fori_loop(0, 31, bisect, jnp.zeros((1, 1), I32))
        gt = bits > thr
        eq = bits == thr
        need = (C - count(gt)).astype(F32)
        e_within, _, e_excl = prefix(eq.astype(BF16))
        eq_rank = e_within - eq.astype(F32) + e_excl
        mask = gt | (eq & (eq_rank < need))
        mask_f = mask.astype(F32)
        within, tot, t_excl = prefix(mask.astype(BF16))
        pos = (within - mask_f + t_excl).astype(I32)
        pos_ref[e] = jnp.where(mask, pos, pos + NOT_SELECTED)

        tot_row = _dot_nt(ones_l, mask.astype(BF16))
        t_incl_row = _dot(tot_row.astype(BF16), upper_b)[0:1, :]
        t_excl_row = t_incl_row - tot_row[0:1, :]
        hi = jnp.floor(t_excl * (1.0 / 256.0))
        rhs = jnp.concatenate([within, hi, t_excl - 256.0 * hi, blk_id], axis=1).astype(BF16)

        def per_chunk(ci, _):
            c0 = pl.multiple_of(ci * CB, CB)
            c_b = (c0 + lax.broadcasted_iota(I32, (CB, nb), 0)).astype(F32)
            onehot = ((t_excl_row <= c_b) & (c_b < t_incl_row)).astype(BF16)
            g = _dot(onehot, rhs)
            c_l = (c0 + lax.broadcasted_iota(I32, (CB, L), 0)).astype(F32)
            inside = (g[:, 0:L] + (256.0 * g[:, L:2 * L] + g[:, 2 * L:3 * L])) <= c_l
            val = (g[:, 3 * L:4 * L] * float(L) + _dot(inside.astype(BF16), ones_l)).astype(I32)
            idx_ref[pl.ds(c0, CB), :] = jnp.where(lane == e, val, idx_ref[pl.ds(c0, CB), :])
            return 0

        lax.fori_loop(0, C // CB, per_chunk, 0)
        return 0

    lax.fori_loop(0, N_EXPERTS, per_expert, 0)


def _select(aff3, C):
    E, nb, L = aff3.shape
    CB = min(C, 512)
    return pl.pallas_call(
        functools.partial(_select_kernel, C=C, CB=CB),
        out_shape=[jax.ShapeDtypeStruct((C, L), I32),
                   jax.ShapeDtypeStruct((E, nb, L), I32)],
        compiler_params=pltpu.CompilerParams(vmem_limit_bytes=VMEM_LIMIT),
        name="ec_select",
    )(aff3)


def _ffn_kernel(idx_ref, x_hbm, wg_ref, wu_ref, wd_ref, o_ref, xbuf, sem, *, tc, nsteps):
    step = pl.program_id(0) * pl.num_programs(1) + pl.program_id(1)
    slot = lax.rem(step, 2)

    def fetch(s, dst_slot):
        for u in range(tc):
            tok = idx_ref[s * tc + u]
            pltpu.make_async_copy(x_hbm.at[pl.ds(tok * SLAB, SLAB)],
                                  xbuf.at[pl.ds((dst_slot * tc + u) * SLAB, SLAB)], sem.at[dst_slot]).start()

    def wait(dst_slot):
        pltpu.make_async_copy(x_hbm.at[pl.ds(0, tc * SLAB)],
                              xbuf.at[pl.ds(dst_slot * tc * SLAB, tc * SLAB)], sem.at[dst_slot]).wait()

    @pl.when(step == 0)
    def _():
        fetch(0, 0)

    wait(slot)
    x = _load_slabs(xbuf, slot * tc, tc)
    fetch(jnp.minimum(step + 1, nsteps - 1), 1 - slot)
    gate = _dot(x, wg_ref[0])
    hid = gate * _sigmoid(gate) * _dot(x, wu_ref[0])
    _store_slabs(o_ref, 0, tc, _dot(hid.astype(BF16), wd_ref[0]))

    @pl.when(step == nsteps - 1)
    def _():
        wait(1 - slot)


def _ffn(idx, xn_slabs, wg, wu, wd, C, tc):
    E, D, F = wg.shape
    nt = C // tc
    return pl.pallas_call(
        functools.partial(_ffn_kernel, tc=tc, nsteps=E * nt),
        grid_spec=pltpu.PrefetchScalarGridSpec(
            num_scalar_prefetch=1,
            grid=(E, nt),
            in_specs=[pl.BlockSpec(memory_space=pl.ANY),
                      pl.BlockSpec((1, D, F), lambda e, i, idx: (e, 0, 0)),
                      pl.BlockSpec((1, D, F), lambda e, i, idx: (e, 0, 0)),
                      pl.BlockSpec((1, F, D), lambda e, i, idx: (e, 0, 0))],
            out_specs=pl.BlockSpec((tc * SLAB, LANES), lambda e, i, idx: (e * nt + i, 0)),
            scratch_shapes=[pltpu.VMEM((2 * tc * SLAB, LANES), I32), pltpu.SemaphoreType.DMA((2,))]),
        out_shape=jax.ShapeDtypeStruct((E * C * SLAB, LANES), I32),
        compiler_params=_cparams(("arbitrary", "arbitrary")),
        name="ec_ffn",
    )(idx, xn_slabs, wg, wu, wd)


COMBINE_W = 64


def _split_bf16(g):
    hi = g.astype(BF16)
    return hi, (g - hi.astype(F32)).astype(BF16)


def _combine_kernel(start_ref, pos_ref, w_ref, h_ref, ye_hbm, g_ref, o_ref, ybuf, ybuf2, acc_ref, sem,
                    *, tm, C, nt):
    E, W = N_EXPERTS, COMBINE_W
    i = pl.program_id(0)
    slot = lax.rem(i, 2)

    def first_slot(tile, e):
        return jnp.minimum(start_ref[tile * E + e], C - W)

    def fetch(tile, dst_slot):
        for e in range(E):
            pltpu.make_async_copy(ye_hbm.at[pl.ds((e * C + first_slot(tile, e)) * SLAB, W * SLAB)],
                                  ybuf.at[pl.ds((dst_slot * E + e) * W * SLAB, W * SLAB)], sem.at[dst_slot]).start()

    def wait(dst_slot):
        pltpu.make_async_copy(ye_hbm.at[pl.ds(0, E * W * SLAB)],
                              ybuf.at[pl.ds(dst_slot * E * W * SLAB, E * W * SLAB)], sem.at[dst_slot]).wait()

    @pl.when(i == 0)
    def _():
        fetch(0, 0)

    fetch(jnp.minimum(i + 1, nt - 1), 1 - slot)
    wait(slot)

    pos = pos_ref[...]
    w = w_ref[...]
    lane = lax.broadcasted_iota(I32, (tm, 2 * W), 1)
    pieces = []
    for p in range(E // 2):
        e0, e1 = 2 * p, 2 * p + 1
        k0 = pos[:, e0:e0 + 1] - first_slot(i, e0)
        k1 = pos[:, e1:e1 + 1] - first_slot(i, e1) + W
        pieces.append(jnp.where((lane == k0) & (lane < W), w[:, e0:e0 + 1],
                                jnp.where((lane == k1) & (lane >= W), w[:, e1:e1 + 1], 0.0)))
    g_hi, g_lo = _split_bf16(jnp.concatenate(pieces, axis=1))
    y = _load_slabs(ybuf, slot * E * W, E * W)
    acc_ref[...] = h_ref[...] + _dot(g_hi, y) + _dot(g_lo, y)

    lane_w = lax.broadcasted_iota(I32, (tm, W), 1)
    for e in range(E):
        s0 = first_slot(i, e)
        n_extra = jnp.maximum((start_ref[(i + 1) * E + e] - s0 + W - 1) // W - 1, 0)

        def extra(kk, _, e=e, s0=s0):
            lo_slot = s0 + (kk + 1) * W
            s2 = jnp.minimum(lo_slot, C - W)
            cp = pltpu.make_async_copy(ye_hbm.at[pl.ds((e * C + s2) * SLAB, W * SLAB)], ybuf2, sem.at[2])
            cp.start()
            cp.wait()
            pe = pos[:, e:e + 1]
            gx_hi, gx_lo = _split_bf16(jnp.where((lane_w == pe - s2) & (pe >= lo_slot), w[:, e:e + 1], 0.0))
            y2 = _load_slabs(ybuf2, 0, W)
            acc_ref[...] += _dot(gx_hi, y2) + _dot(gx_lo, y2)
            return 0

        lax.fori_loop(0, n_extra, extra, 0)

    o_ref[...] = _rms(acc_ref[...], g_ref[...])

    @pl.when(i == nt - 1)
    def _():
        wait(1 - slot)


def _combine(start, pos_t, w_t, h2, ye_slabs, g, C, tm):
    n, D = h2.shape
    nt = n // tm
    E, W = N_EXPERTS, COMBINE_W
    assert C >= W
    return pl.pallas_call(
        functools.partial(_combine_kernel, tm=tm, C=C, nt=nt),
        grid_spec=pltpu.PrefetchScalarGridSpec(
            num_scalar_prefetch=1,
            grid=(nt,),
            in_specs=[pl.BlockSpec((tm, E), lambda i, st: (i, 0)),
                      pl.BlockSpec((tm, E), lambda i, st: (i, 0)),
                      pl.BlockSpec((tm, D), lambda i, st: (i, 0)),
                      pl.BlockSpec(memory_space=pl.ANY),
                      pl.BlockSpec((1, D), lambda i, st: (0, 0))],
            out_specs=pl.BlockSpec((tm, D), lambda i, st: (i, 0)),
            scratch_shapes=[pltpu.VMEM((2 * E * W * SLAB, LANES), I32),
                            pltpu.VMEM((W * SLAB, LANES), I32),
                            pltpu.VMEM((tm, D), F32),
                            pltpu.SemaphoreType.DMA((3,))]),
        out_shape=jax.ShapeDtypeStruct((n, D), F32),
        compiler_params=_cparams(("arbitrary",)),
        name="ec_combine",
    )(start, pos_t, w_t, h2, ye_slabs, g)


def _pick(n, prefs):
    for p in prefs:
        if n % p == 0:
            return p
    raise ValueError(f"no tile in {prefs} divides {n}")


def _prep_weights(norm_mix, w_in, q_norm, w_uq, kv_norm, w_ukv, conv_w, conv_b, w_rg_a, b_rg_a, w_rg_x, b_rg_x,
                  rg_lambda, w_out, norm_cross, norm_mem, w_cq, w_ck, w_cv, w_co, norm_ffn, w_router,
                  w_gate, w_up, w_down, norm_final):
    swap = (jnp.arange(QK_ROPE) + QK_ROPE // 2) % QK_ROPE
    o = Q_LORA + KV_LORA
    k_r = w_in[:, o:o + QK_ROPE]
    win = jnp.concatenate([w_in[:, :o], w_in[:, o + QK_ROPE:], k_r, k_r[:, swap]], axis=1).astype(BF16)
    uq = w_uq.reshape(Q_LORA, MLA_HEADS, QK_NOPE + QK_ROPE)
    rope = uq[:, :, QK_NOPE:]
    wuq = jnp.concatenate([uq[:, :, :QK_NOPE], rope, rope[:, :, swap]], axis=2).reshape(Q_LORA, MLA_HEADS * HEAD_QK)
    wgate = jnp.concatenate([w_rg_a[0], w_rg_a[1], w_rg_x[0], w_rg_x[1]], axis=2).astype(BF16)
    bgate = jnp.concatenate([b_rg_a[0], b_rg_a[1], b_rg_x[0], b_rg_x[1]], axis=1)[:, None, :]
    row = lambda v: v.reshape(1, -1)
    return dict(
        gmix=row(norm_mix), win=win, qg=row(q_norm), wuq=wuq.astype(BF16), kvg=row(kv_norm), wukv=w_ukv.astype(BF16),
        conv_w=conv_w, conv_b=row(conv_b), wgate=wgate, bgate=bgate, lam=rg_lambda,
        wout=w_out.astype(BF16), gcross=row(norm_cross), gmem=row(norm_mem),
        wcq=w_cq.astype(BF16), wck=w_ck.astype(BF16), wcv=w_cv.astype(BF16), wco=w_co.astype(BF16),
        gffn=row(norm_ffn), wr_t=w_router.T.astype(BF16),
        wg=w_gate.astype(BF16), wu=w_up.astype(BF16), wd=w_down.astype(BF16), gfinal=row(norm_final))


def _rope_mults(S):
    inv = ROPE_THETA ** (-jnp.arange(0, QK_ROPE, 2, dtype=F32) / QK_ROPE)
    ang = jnp.arange(S, dtype=F32)[:, None] * inv[None, :]
    cos, sin = jnp.cos(ang), jnp.sin(ang)
    kmult = jnp.concatenate([cos, cos, -sin, sin], axis=1)
    scale = (QK_NOPE + QK_ROPE) ** -0.5 * 1.4426950408889634
    qmult = scale * jnp.concatenate([jnp.ones((S, QK_NOPE), F32), kmult], axis=1)
    return qmult, kmult


def _run(x, mem, W):
    B, S, D = x.shape
    n = B * S
    C = EC_CAPACITY_FACTOR * n // N_EXPERTS
    qmult, kmult = _rope_mults(S)

    q, k, v, xbr, gbr = _inproj(x, W["gmix"], W["win"], W["qg"], W["wuq"], W["kvg"], W["wukv"], qmult, kmult,
                                tm=_pick(S, (256, 128)))
    o_mla = _attention(q, k, v, tq=_pick(S, (1024, 512, 256, 128)), tk=_pick(S, (512, 256, 128)))
    o_rg = _rglru(xbr, gbr, W["conv_w"], W["conv_b"], W["wgate"], W["bgate"], W["lam"], R=_pick(S, (256, 128)))
    h1, hn = _outproj(o_mla.reshape(n, MLA_WIDTH), o_rg.reshape(n, RG_WIDTH), x.reshape(n, D), W["wout"],
                      W["gcross"], tm=_pick(n, (512, 256, 128)))
    kmem, vmem = _memkv(mem, W["gmem"], W["wck"], W["wcv"])
    h2, xn, aff_t = _cross(hn.reshape(B, S, D), h1.reshape(B, S, D), kmem, vmem, W["wcq"], W["wco"], W["gffn"],
                           W["wr_t"], tm=_pick(S, (256, 128)))

    nb = max(n // LANES, LANES)
    aff3 = jnp.pad(aff_t, ((0, 0), (0, nb * LANES - n)), constant_values=-1.0).reshape(N_EXPERTS, nb, LANES)
    idx_t, posm = _select(aff3, C)
    idx = idx_t[:, :N_EXPERTS].T.reshape(N_EXPERTS * C)
    ye = _ffn(idx, xn, W["wg"], W["wu"], W["wd"], C, tc=_pick(C, (256, 128)))

    tm = _pick(n, (256, 128))
    posm = posm.reshape(N_EXPERTS, nb * LANES)[:, :n]
    first = posm[:, ::tm]
    first = jnp.where(first < 0, first - NOT_SELECTED, first)
    start = jnp.concatenate([first.T, jnp.full((1, N_EXPERTS), C, I32)], axis=0).reshape(-1)
    y = _combine(start, posm.T, aff_t.T, h2.reshape(n, D), ye, W["gfinal"], C, tm=tm)
    return y.reshape(B, S, D)


def kernel(x_prompt, x_sample, mem_prompt, mem_sample, norm_mix, w_in, q_norm, w_uq, kv_norm, w_ukv, conv_w, conv_b,
           w_rg_a, b_rg_a, w_rg_x, b_rg_x, rg_lambda, w_out, norm_cross, norm_mem, w_cq, w_ck, w_cv, w_co, norm_ffn,
           w_router, w_gate, w_up, w_down, norm_final):
    W = _prep_weights(norm_mix[0], w_in[0], q_norm[0], w_uq[0], kv_norm[0], w_ukv[0], conv_w[0], conv_b[0],
                      w_rg_a[0], b_rg_a[0], w_rg_x[0], b_rg_x[0], rg_lambda[0], w_out[0], norm_cross[0],
                      norm_mem[0], w_cq[0], w_ck[0], w_cv[0], w_co[0], norm_ffn[0], w_router[0],
                      w_gate[0], w_up[0], w_down[0], norm_final)
    return (_run(x_prompt, mem_prompt, W), _run(x_sample, mem_sample, W))
```

```python
import functools

import jax
import jax.numpy as jnp
from jax import lax
from jax.experimental import pallas as pl
from jax.experimental.pallas import tpu as pltpu

F32 = jnp.float32
BF16 = jnp.bfloat16
I32 = jnp.int32

D_MODEL = 2048
MLA_HEADS = 8
QK_NOPE = 128
QK_ROPE = 64
V_HEAD = 128
Q_LORA = 512
KV_LORA = 512
MLA_WIDTH = MLA_HEADS * V_HEAD
RG_WIDTH = D_MODEL - MLA_WIDTH
RG_BLOCKS = 8
RG_BLOCK = RG_WIDTH // RG_BLOCKS
LRU_C = 8.0
MEM_TOKENS = 256
X_HEADS = 4
X_HEAD = D_MODEL // X_HEADS
N_EXPERTS = 16
EC_CAPACITY_FACTOR = 2
EXPERT_FF = 1408
ROPE_THETA = 10000.0
EPS = 1e-6

LANES = 128
HEAD_QK = 2 * LANES
VMEM_LIMIT = 56 * 1024 * 1024


def _cparams(sem, vmem=VMEM_LIMIT):
    return pltpu.CompilerParams(dimension_semantics=sem, vmem_limit_bytes=vmem)


def _rms(x, g):
    ms = jnp.mean(x * x, axis=-1, keepdims=True)
    return x * lax.rsqrt(ms + EPS) * g


def _sigmoid(x):
    return 0.5 * jnp.tanh(0.5 * x) + 0.5


def _dot(a, b):
    return jnp.dot(a, b, preferred_element_type=F32)


def _dot_nt(a, b):
    return lax.dot_general(a, b, (((1,), (1,)), ((), ())), preferred_element_type=F32)


SLAB = D_MODEL // (2 * LANES)
HI_MASK = -65536


def _store_slabs(ref, row0, m, y):
    for j in range(SLAB):
        lo = pltpu.bitcast(y[:, j * LANES:(j + 1) * LANES].astype(BF16).astype(F32), I32)
        hi = pltpu.bitcast(y[:, (j + SLAB) * LANES:(j + SLAB + 1) * LANES].astype(BF16).astype(F32), I32)
        ref[pl.ds(row0 * SLAB + j, m, stride=SLAB), :] = (hi & HI_MASK) | lax.shift_right_logical(lo, 16)


def _load_slabs(ref, row0, m):
    lo, hi = [], []
    for j in range(SLAB):
        w = ref[pl.ds(row0 * SLAB + j, m, stride=SLAB), :]
        lo.append(pltpu.bitcast(lax.shift_left(w, 16), F32).astype(BF16))
        hi.append(pltpu.bitcast(w & HI_MASK, F32).astype(BF16))
    return jnp.concatenate(lo + hi, axis=1)


def _inproj_kernel(x_ref, gmix_ref, win_ref, qg_ref, wuq_ref, kvg_ref, wukv_ref,
                   qmult_ref, kmult_ref, q_ref, k_ref, v_ref, xbr_ref, gbr_ref):
    hn = _rms(x_ref[0], gmix_ref[...]).astype(BF16)
    cq = _dot(hn, win_ref[:, 0:Q_LORA])
    ckv = _dot(hn, win_ref[:, Q_LORA:Q_LORA + KV_LORA])
    o = Q_LORA + KV_LORA
    xbr_ref[0] = _dot(hn, win_ref[:, o:o + RG_WIDTH])
    gbr_ref[0] = _dot(hn, win_ref[:, o + RG_WIDTH:o + 2 * RG_WIDTH])
    kr = _dot(hn, win_ref[:, o + 2 * RG_WIDTH:o + 2 * RG_WIDTH + LANES])
    tk = kr * kmult_ref[...]
    rkk = (tk + pltpu.roll(tk, LANES // 2, 1)).astype(BF16)

    q = _dot(_rms(cq, qg_ref[...]).astype(BF16), wuq_ref[...])
    kv = _dot(_rms(ckv, kvg_ref[...]).astype(BF16), wukv_ref[...])
    qm = qmult_ref[...]
    for h in range(MLA_HEADS):
        c = h * HEAD_QK
        q_ref[0, h] = (q[:, c:c + HEAD_QK] * qm).astype(BF16)
        k_ref[0, h, :, 0:LANES] = kv[:, c:c + LANES].astype(BF16)
        k_ref[0, h, :, LANES:HEAD_QK] = rkk
        v_ref[0, h, :, 0:LANES] = kv[:, c + LANES:c + HEAD_QK].astype(BF16)
        v_ref[0, h, :, LANES:HEAD_QK] = jnp.ones((kv.shape[0], LANES), BF16)


def _inproj(x, gmix, win, qg, wuq, kvg, wukv, qmult, kmult, tm):
    B, S, D = x.shape
    full = lambda a: pl.BlockSpec(a.shape, lambda b, i: (0,) * a.ndim)
    return pl.pallas_call(
        _inproj_kernel,
        grid=(B, S // tm),
        in_specs=[pl.BlockSpec((1, tm, D), lambda b, i: (b, i, 0)),
                  full(gmix), full(win), full(qg), full(wuq), full(kvg), full(wukv),
                  pl.BlockSpec((tm, HEAD_QK), lambda b, i: (i, 0)),
                  pl.BlockSpec((tm, LANES), lambda b, i: (i, 0))],
        out_specs=[pl.BlockSpec((1, MLA_HEADS, tm, HEAD_QK), lambda b, i: (b, 0, i, 0)),
                   pl.BlockSpec((1, MLA_HEADS, tm, HEAD_QK), lambda b, i: (b, 0, i, 0)),
                   pl.BlockSpec((1, MLA_HEADS, tm, 2 * V_HEAD), lambda b, i: (b, 0, i, 0)),
                   pl.BlockSpec((1, tm, RG_WIDTH), lambda b, i: (b, i, 0)),
                   pl.BlockSpec((1, tm, RG_WIDTH), lambda b, i: (b, i, 0))],
        out_shape=[jax.ShapeDtypeStruct((B, MLA_HEADS, S, HEAD_QK), BF16),
                   jax.ShapeDtypeStruct((B, MLA_HEADS, S, HEAD_QK), BF16),
                   jax.ShapeDtypeStruct((B, MLA_HEADS, S, 2 * V_HEAD), BF16),
                   jax.ShapeDtypeStruct((B, S, RG_WIDTH), F32),
                   jax.ShapeDtypeStruct((B, S, RG_WIDTH), F32)],
        compiler_params=_cparams(("parallel", "parallel")),
        name="inproj",
    )(x, gmix, win, qg, wuq, kvg, wukv, qmult, kmult)


def _attn_kernel(q_ref, k_ref, v_ref, o_ref, *, tk):
    q = q_ref[0, 0]
    tq = q.shape[0]
    nk = k_ref.shape[2] // tk

    def body(j, carry):
        m, acc = carry
        off = pl.multiple_of(j * tk, tk)
        s = _dot_nt(q, k_ref[0, 0, pl.ds(off, tk), :])
        m_new = jnp.maximum(m, jnp.max(s, axis=1, keepdims=True))
        alpha = jnp.exp2(m - m_new)
        p = jnp.exp2(s - m_new)
        acc = alpha * acc + _dot(p.astype(BF16), v_ref[0, 0, pl.ds(off, tk), :])
        return m_new, acc

    init = (jnp.full((tq, 1), -jnp.inf, F32), jnp.zeros((tq, 2 * V_HEAD), F32))
    _, acc = lax.fori_loop(0, nk, body, init, unroll=8)
    o_ref[0] = (acc[:, :V_HEAD] / acc[:, V_HEAD:]).astype(BF16)


def _attention(q, k, v, tq, tk):
    B, H, S, _ = q.shape
    return pl.pallas_call(
        functools.partial(_attn_kernel, tk=tk),
        grid=(B, H, S // tq),
        in_specs=[pl.BlockSpec((1, 1, tq, HEAD_QK), lambda b, h, i: (b, h, i, 0)),
                  pl.BlockSpec((1, 1, S, HEAD_QK), lambda b, h, i: (b, h, 0, 0)),
                  pl.BlockSpec((1, 1, S, 2 * V_HEAD), lambda b, h, i: (b, h, 0, 0))],
        out_specs=pl.BlockSpec((1, tq, V_HEAD), lambda b, h, i: (b, i, h)),
        out_shape=jax.ShapeDtypeStruct((B, S, H * V_HEAD), BF16),
        compiler_params=_cparams(("parallel", "parallel", "parallel")),
        name="mla_attention",
    )(q, k, v)


def _rg_kernel(x_ref, g_ref, cw_ref, cb_ref, wg_ref, bg_ref, lam_ref, o_ref,
               hf_ref, ab_ref, ub_ref, *, R):
    S = x_ref.shape[1]
    nch = S // R
    W = RG_BLOCK
    cw = cw_ref[...]
    cb = cb_ref[...]
    ls = LRU_C * jax.nn.log_sigmoid(lam_ref[...])
    rows = lax.broadcasted_iota(I32, (R, W), 0)
    next_rows = R + 16

    def gates(c):
        r0 = pl.multiple_of(c * R, R)
        cur = x_ref[0, pl.ds(r0, R), :]
        prev8 = x_ref[0, pl.ds(pl.multiple_of(jnp.maximum(r0 - 8, 0), 8), 8), :]
        prev8 = jnp.where(c > 0, prev8, 0.0)
        next8 = x_ref[0, pl.ds(pl.multiple_of(jnp.minimum(r0 + R, S - 8), 8), 8), :]
        next8 = jnp.where(c < nch - 1, next8, 0.0)
        ext = jnp.concatenate([prev8, cur, next8], axis=0)
        xm1 = pltpu.roll(ext, 1, 0)[8:8 + R]
        xp1 = pltpu.roll(ext, next_rows - 1, 0)[8:8 + R]
        xp2 = pltpu.roll(ext, next_rows - 2, 0)[8:8 + R]
        xc = cb + xm1 * cw[0:1] + cur * cw[1:2] + xp1 * cw[2:3] + xp2 * cw[3:4]
        sg = _sigmoid(_dot(xc.astype(BF16), wg_ref[0]) + bg_ref[0])
        out = []
        for n in range(2):
            log_a = sg[:, n * W:(n + 1) * W] * ls[n:n + 1]
            a = jnp.exp(log_a)
            t = jnp.tanh(log_a)
            p = -2.0 * t
            root = jnp.where(p > 0.0, p * lax.rsqrt(p * (1.0 - t)), 0.0)
            u = root * sg[:, (2 + n) * W:(3 + n) * W] * xc
            out.append((a, u))
        return out

    def scan(a, u, reverse):
        d = 1
        while d < 8:
            if reverse:
                keep = rows < R - d
                sh = R - d
            else:
                keep = rows >= d
                sh = d
            a_s = jnp.where(keep, pltpu.roll(a, sh, 0), 1.0)
            u_s = jnp.where(keep, pltpu.roll(u, sh, 0), 0.0)
            u = a * u_s + u
            a = a * a_s
            d *= 2
        while d < R:
            if reverse:
                u = jnp.concatenate([a[:R - d] * u[d:] + u[:R - d], u[R - d:]], axis=0)
                a = jnp.concatenate([a[:R - d] * a[d:], a[R - d:]], axis=0)
            else:
                u = jnp.concatenate([u[:d], a[d:] * u[:R - d] + u[d:]], axis=0)
                a = jnp.concatenate([a[:d], a[d:] * a[:R - d]], axis=0)
            d *= 2
        return a, u

    def fwd(c, carry):
        r0 = pl.multiple_of(c * R, R)
        (a_f, u_f), (a_b, u_b) = gates(c)
        ab_ref[pl.ds(r0, R), :] = a_b
        ub_ref[pl.ds(r0, R), :] = u_b
        acum, hloc = scan(a_f, u_f, False)
        h = acum * carry + hloc
        hf_ref[pl.ds(r0, R), :] = h
        return h[R - 1:R, :]

    lax.fori_loop(0, nch, fwd, jnp.zeros((1, W), F32))

    def bwd(i, carry):
        c = nch - 1 - i
        r0 = pl.multiple_of(c * R, R)
        acum, hloc = scan(ab_ref[pl.ds(r0, R), :], ub_ref[pl.ds(r0, R), :], True)
        h = acum * carry + hloc
        gate = jax.nn.gelu(g_ref[0, pl.ds(r0, R), :], approximate=True)
        o_ref[0, pl.ds(r0, R), :] = ((hf_ref[pl.ds(r0, R), :] + h) * gate).astype(BF16)
        return h[0:1, :]

    lax.fori_loop(0, nch, bwd, jnp.zeros((1, W), F32))


def _rglru(xbr, gbr, conv_w, conv_b, wgate, bgate, lam, R):
    B, S, _ = xbr.shape
    W = RG_BLOCK
    return pl.pallas_call(
        functools.partial(_rg_kernel, R=R),
        grid=(B, RG_BLOCKS),
        in_specs=[pl.BlockSpec((1, S, W), lambda b, g: (b, 0, g)),
                  pl.BlockSpec((1, S, W), lambda b, g: (b, 0, g)),
                  pl.BlockSpec((4, W), lambda b, g: (0, g)),
                  pl.BlockSpec((1, W), lambda b, g: (0, g)),
                  pl.BlockSpec((1, W, 4 * W), lambda b, g: (g, 0, 0)),
                  pl.BlockSpec((1, 1, 4 * W), lambda b, g: (g, 0, 0)),
                  pl.BlockSpec((2, W), lambda b, g: (0, g))],
        out_specs=pl.BlockSpec((1, S, W), lambda b, g: (b, 0, g)),
        out_shape=jax.ShapeDtypeStruct((B, S, RG_WIDTH), BF16),
        scratch_shapes=[pltpu.VMEM((S, W), F32), pltpu.VMEM((S, W), F32), pltpu.VMEM((S, W), F32)],
        compiler_params=_cparams(("parallel", "parallel")),
        name="rglru",
    )(xbr, gbr, conv_w, conv_b, wgate, bgate, lam)


def _outproj_kernel(om_ref, og_ref, x_ref, wout_ref, g_ref, h_ref, hn_ref):
    h = x_ref[...] + _dot(om_ref[...], wout_ref[0:MLA_WIDTH, :]) + _dot(og_ref[...], wout_ref[MLA_WIDTH:, :])
    h_ref[...] = h
    hn_ref[...] = _rms(h, g_ref[...]).astype(BF16)


def _outproj(o_mla, o_rg, x, wout, g, tm):
    M, D = x.shape
    return pl.pallas_call(
        _outproj_kernel,
        grid=(M // tm,),
        in_specs=[pl.BlockSpec((tm, MLA_WIDTH), lambda i: (i, 0)),
                  pl.BlockSpec((tm, RG_WIDTH), lambda i: (i, 0)),
                  pl.BlockSpec((tm, D), lambda i: (i, 0)),
                  pl.BlockSpec((D, D), lambda i: (0, 0)),
                  pl.BlockSpec((1, D), lambda i: (0, 0))],
        out_specs=[pl.BlockSpec((tm, D), lambda i: (i, 0)), pl.BlockSpec((tm, D), lambda i: (i, 0))],
        out_shape=[jax.ShapeDtypeStruct((M, D), F32), jax.ShapeDtypeStruct((M, D), BF16)],
        compiler_params=_cparams(("parallel",)),
        name="outproj",
    )(o_mla, o_rg, x, wout, g)


def _memkv_kernel(mem_ref, g_ref, wk_ref, wv_ref, k_ref, v_ref):
    mn = _rms(mem_ref[0], g_ref[...]).astype(BF16)
    k_ref[0] = _dot(mn, wk_ref[...]).astype(BF16)
    v_ref[0] = _dot(mn, wv_ref[...]).astype(BF16)


def _memkv(mem, g, wk, wv):
    B, Mt, D = mem.shape
    return pl.pallas_call(
        _memkv_kernel,
        grid=(B,),
        in_specs=[pl.BlockSpec((1, Mt, D), lambda b: (b, 0, 0)),
                  pl.BlockSpec((1, D), lambda b: (0, 0)),
                  pl.BlockSpec((D, D), lambda b: (0, 0)),
                  pl.BlockSpec((D, D), lambda b: (0, 0))],
        out_specs=[pl.BlockSpec((1, Mt, D), lambda b: (b, 0, 0)), pl.BlockSpec((1, Mt, D), lambda b: (b, 0, 0))],
        out_shape=[jax.ShapeDtypeStruct((B, Mt, D), BF16), jax.ShapeDtypeStruct((B, Mt, D), BF16)],
        compiler_params=_cparams(("parallel",)),
        name="memkv",
    )(mem, g, wk, wv)


def _cross_kernel(hn_ref, h_ref, k_ref, v_ref, wq_ref, wo_ref, g_ref, wr_ref, h2_ref, xn_ref, aff_ref):
    q = _dot(hn_ref[0], wq_ref[...]).astype(BF16)
    outs = []
    for hd in range(X_HEADS):
        c = hd * X_HEAD
        s = _dot_nt(q[:, c:c + X_HEAD], k_ref[0, :, c:c + X_HEAD]) * (X_HEAD ** -0.5)
        e = jnp.exp(s - jnp.max(s, axis=1, keepdims=True))
        p = e * (1.0 / jnp.sum(e, axis=1, keepdims=True))
        outs.append(_dot(p.astype(BF16), v_ref[0, :, c:c + X_HEAD]).astype(BF16))
    h2 = h_ref[0] + _dot(jnp.concatenate(outs, axis=1), wo_ref[...])
    h2_ref[0] = h2
    xn = _rms(h2, g_ref[...])
    _store_slabs(xn_ref, 0, xn.shape[0], xn)
    logits = _dot_nt(wr_ref[...], xn.astype(BF16))
    e = jnp.exp(logits - jnp.max(logits, axis=0, keepdims=True))
    aff_ref[...] = e / jnp.sum(e, axis=0, keepdims=True)


def _cross(hn, h1, kmem, vmem, wq, wo, g, wr_t, tm):
    B, S, D = h1.shape
    nt = S // tm
    return pl.pallas_call(
        _cross_kernel,
        grid=(B, nt),
        in_specs=[pl.BlockSpec((1, tm, D), lambda b, i: (b, i, 0)),
                  pl.BlockSpec((1, tm, D), lambda b, i: (b, i, 0)),
                  pl.BlockSpec((1, MEM_TOKENS, D), lambda b, i: (b, 0, 0)),
                  pl.BlockSpec((1, MEM_TOKENS, D), lambda b, i: (b, 0, 0)),
                  pl.BlockSpec((D, D), lambda b, i: (0, 0)),
                  pl.BlockSpec((D, D), lambda b, i: (0, 0)),
                  pl.BlockSpec((1, D), lambda b, i: (0, 0)),
                  pl.BlockSpec((N_EXPERTS, D), lambda b, i: (0, 0))],
        out_specs=[pl.BlockSpec((1, tm, D), lambda b, i: (b, i, 0)),
                   pl.BlockSpec((tm * SLAB, LANES), lambda b, i: (b * nt + i, 0)),
                   pl.BlockSpec((N_EXPERTS, tm), lambda b, i: (0, b * nt + i))],
        out_shape=[jax.ShapeDtypeStruct((B, S, D), F32),
                   jax.ShapeDtypeStruct((B * S * SLAB, LANES), I32),
                   jax.ShapeDtypeStruct((N_EXPERTS, B * S), F32)],
        compiler_params=_cparams(("parallel", "parallel")),
        name="cross_router",
    )(hn, h1, kmem, vmem, wq, wo, g, wr_t)


NOT_SELECTED = -(1 << 20)


def _select_kernel(aff_ref, idx_ref, pos_ref, *, C, CB):
    nb = aff_ref.shape[1]
    L = LANES
    li = lax.broadcasted_iota(I32, (L, L), 0)
    lj = lax.broadcasted_iota(I32, (L, L), 1)
    upper_l = (li <= lj).astype(BF16)
    ones_l = jnp.ones((L, L), BF16)
    bi = lax.broadcasted_iota(I32, (nb, nb), 0)
    bj = lax.broadcasted_iota(I32, (nb, nb), 1)
    lower_strict = (bj < bi).astype(BF16)
    upper_b = (bi <= bj).astype(BF16)
    blk_id = lax.broadcasted_iota(I32, (nb, L), 0).astype(F32)
    lane = lax.broadcasted_iota(I32, (CB, L), 1)

    idx_ref[...] = jnp.zeros_like(idx_ref)

    def count(m):
        return jnp.sum(jnp.sum(m.astype(I32), axis=1, keepdims=True), axis=0, keepdims=True)

    def prefix(m_b):
        within = _dot(m_b, upper_l)
        tot = _dot(m_b, ones_l)
        t_excl = _dot(lower_strict, tot.astype(BF16))
        return within, tot, t_excl

    def per_expert(e, _):
        a = aff_ref[e]
        bits = pltpu.bitcast(a, I32)

        def bisect(i, thr):
            cand = thr | jnp.left_shift(jnp.int32(1), 30 - i)
            return jnp.where(count(bits >= cand) >= C, cand, thr)

        thr = lax.fori_loop(0, 31, bisect, jnp.zeros((1, 1), I32))
        gt = bits > thr
        eq = bits == thr
        need = (C - count(gt)).astype(F32)
        e_within, _, e_excl = prefix(eq.astype(BF16))
        eq_rank = e_within - eq.astype(F32) + e_excl
        mask = gt | (eq & (eq_rank < need))
        mask_f = mask.astype(F32)
        within, tot, t_excl = prefix(mask.astype(BF16))
        pos = (within - mask_f + t_excl).astype(I32)
        pos_ref[e] = jnp.where(mask, pos, pos + NOT_SELECTED)

        tot_row = _dot_nt(ones_l, mask.astype(BF16))
        t_incl_row = _dot(tot_row.astype(BF16), upper_b)[0:1, :]
        t_excl_row = t_incl_row - tot_row[0:1, :]
        hi = jnp.floor(t_excl * (1.0 / 256.0))
        rhs = jnp.concatenate([within, hi, t_excl - 256.0 * hi, blk_id], axis=1).astype(BF16)

        def per_chunk(ci, _):
            c0 = pl.multiple_of(ci * CB, CB)
            c_b = (c0 + lax.broadcasted_iota(I32, (CB, nb), 0)).astype(F32)
            onehot = ((t_excl_row <= c_b) & (c_b < t_incl_row)).astype(BF16)
            g = _dot(onehot, rhs)
            c_l = (c0 + lax.broadcasted_iota(I32, (CB, L), 0)).astype(F32)
            inside = (g[:, 0:L] + (256.0 * g[:, L:2 * L] + g[:, 2 * L:3 * L])) <= c_l
            val = (g[:, 3 * L:4 * L] * float(L) + _dot(inside.astype(BF16), ones_l)).astype(I32)
            idx_ref[pl.ds(c0, CB), :] = jnp.where(lane == e, val, idx_ref[pl.ds(c0, CB), :])
            return 0

        lax.fori_loop(0, C // CB, per_chunk, 0)
        return 0

    lax.fori_loop(0, N_EXPERTS, per_expert, 0)


def _select(aff3, C):
    E, nb, L = aff3.shape
    CB = min(C, 512)
    return pl.pallas_call(
        functools.partial(_select_kernel, C=C, CB=CB),
        out_shape=[jax.ShapeDtypeStruct((C, L), I32),
                   jax.ShapeDtypeStruct((E, nb, L), I32)],
        compiler_params=pltpu.CompilerParams(vmem_limit_bytes=VMEM_LIMIT),
        name="ec_select",
    )(aff3)


def _ffn_kernel(idx_ref, x_hbm, wg_ref, wu_ref, wd_ref, o_ref, xbuf, sem, *, tc, nsteps):
    step = pl.program_id(0) * pl.num_programs(1) + pl.program_id(1)
    slot = lax.rem(step, 2)

    def fetch(s, dst_slot):
        for u in range(tc):
            tok = idx_ref[s * tc + u]
            pltpu.make_async_copy(x_hbm.at[pl.ds(tok * SLAB, SLAB)],
                                  xbuf.at[pl.ds((dst_slot * tc + u) * SLAB, SLAB)], sem.at[dst_slot]).start()

    def wait(dst_slot):
        pltpu.make_async_copy(x_hbm.at[pl.ds(0, tc * SLAB)],
                              xbuf.at[pl.ds(dst_slot * tc * SLAB, tc * SLAB)], sem.at[dst_slot]).wait()

    @pl.when(step == 0)
    def _():
        fetch(0, 0)

    fetch(jnp.minimum(step + 1, nsteps - 1), 1 - slot)
    wait(slot)
    x = _load_slabs(xbuf, slot * tc, tc)
    gate = _dot(x, wg_ref[0])
    hid = gate * _sigmoid(gate) * _dot(x, wu_ref[0])
    _store_slabs(o_ref, 0, tc, _dot(hid.astype(BF16), wd_ref[0]))

    @pl.when(step == nsteps - 1)
    def _():
        wait(1 - slot)


def _ffn(idx, xn_slabs, wg, wu, wd, C, tc):
    E, D, F = wg.shape
    nt = C // tc
    return pl.pallas_call(
        functools.partial(_ffn_kernel, tc=tc, nsteps=E * nt),
        grid_spec=pltpu.PrefetchScalarGridSpec(
            num_scalar_prefetch=1,
            grid=(E, nt),
            in_specs=[pl.BlockSpec(memory_space=pl.ANY),
                      pl.BlockSpec((1, D, F), lambda e, i, idx: (e, 0, 0)),
                      pl.BlockSpec((1, D, F), lambda e, i, idx: (e, 0, 0)),
                      pl.BlockSpec((1, F, D), lambda e, i, idx: (e, 0, 0))],
            out_specs=pl.BlockSpec((tc * SLAB, LANES), lambda e, i, idx: (e * nt + i, 0)),
            scratch_shapes=[pltpu.VMEM((2 * tc * SLAB, LANES), I32), pltpu.SemaphoreType.DMA((2,))]),
        out_shape=jax.ShapeDtypeStruct((E * C * SLAB, LANES), I32),
        compiler_params=_cparams(("arbitrary", "arbitrary")),
        name="ec_ffn",
    )(idx, xn_slabs, wg, wu, wd)


COMBINE_W = 64


def _split_bf16(g):
    hi = g.astype(BF16)
    return hi, (g - hi.astype(F32)).astype(BF16)


def _combine_kernel(start_ref, pos_ref, w_ref, h_ref, ye_hbm, g_ref, o_ref, ybuf, ybuf2, acc_ref, sem,
                    *, tm, C, nt):
    E, W = N_EXPERTS, COMBINE_W
    i = pl.program_id(0)
    slot = lax.rem(i, 2)

    def first_slot(tile, e):
        return jnp.minimum(start_ref[tile * E + e], C - W)

    def fetch(tile, dst_slot):
        for e in range(E):
            pltpu.make_async_copy(ye_hbm.at[pl.ds((e * C + first_slot(tile, e)) * SLAB, W * SLAB)],
                                  ybuf.at[pl.ds((dst_slot * E + e) * W * SLAB, W * SLAB)], sem.at[dst_slot]).start()

    def wait(dst_slot):
        pltpu.make_async_copy(ye_hbm.at[pl.ds(0, E * W * SLAB)],
                              ybuf.at[pl.ds(dst_slot * E * W * SLAB, E * W * SLAB)], sem.at[dst_slot]).wait()

    @pl.when(i == 0)
    def _():
        fetch(0, 0)

    fetch(jnp.minimum(i + 1, nt - 1), 1 - slot)
    wait(slot)

    pos = pos_ref[...]
    w = w_ref[...]
    lane = lax.broadcasted_iota(I32, (tm, 2 * W), 1)
    pieces = []
    for p in range(E // 2):
        e0, e1 = 2 * p, 2 * p + 1
        k0 = pos[:, e0:e0 + 1] - first_slot(i, e0)
        k1 = pos[:, e1:e1 + 1] - first_slot(i, e1) + W
        pieces.append(jnp.where((lane == k0) & (lane < W), w[:, e0:e0 + 1],
                                jnp.where((lane == k1) & (lane >= W), w[:, e1:e1 + 1], 0.0)))
    g_hi, g_lo = _split_bf16(jnp.concatenate(pieces, axis=1))
    y = _load_slabs(ybuf, slot * E * W, E * W)
    acc_ref[...] = h_ref[...] + _dot(g_hi, y) + _dot(g_lo, y)

    lane_w = lax.broadcasted_iota(I32, (tm, W), 1)
    for e in range(E):
        s0 = first_slot(i, e)
        n_extra = jnp.maximum((start_ref[(i + 1) * E + e] - s0 + W - 1) // W - 1, 0)

        def extra(kk, _, e=e, s0=s0):
            lo_slot = s0 + (kk + 1) * W
            s2 = jnp.minimum(lo_slot, C - W)
            cp = pltpu.make_async_copy(ye_hbm.at[pl.ds((e * C + s2) * SLAB, W * SLAB)], ybuf2, sem.at[2])
            cp.start()
            cp.wait()
            pe = pos[:, e:e + 1]
            gx_hi, gx_lo = _split_bf16(jnp.where((lane_w == pe - s2) & (pe >= lo_slot), w[:, e:e + 1], 0.0))
            y2 = _load_slabs(ybuf2, 0, W)
            acc_ref[...] += _dot(gx_hi, y2) + _dot(gx_lo, y2)
            return 0

        lax.fori_loop(0, n_extra, extra, 0)

    o_ref[...] = _rms(acc_ref[...], g_ref[...])

    @pl.when(i == nt - 1)
    def _():
        wait(1 - slot)


def _combine(start, pos_t, w_t, h2, ye_slabs, g, C, tm):
    n, D = h2.shape
    nt = n // tm
    E, W = N_EXPERTS, COMBINE_W
    assert C >= W
    return pl.pallas_call(
        functools.partial(_combine_kernel, tm=tm, C=C, nt=nt),
        grid_spec=pltpu.PrefetchScalarGridSpec(
            num_scalar_prefetch=1,
            grid=(nt,),
            in_specs=[pl.BlockSpec((tm, E), lambda i, st: (i, 0)),
                      pl.BlockSpec((tm, E), lambda i, st: (i, 0)),
                      pl.BlockSpec((tm, D), lambda i, st: (i, 0)),
                      pl.BlockSpec(memory_space=pl.ANY),
                      pl.BlockSpec((1, D), lambda i, st: (0, 0))],
            out_specs=pl.BlockSpec((tm, D), lambda i, st: (i, 0)),
            scratch_shapes=[pltpu.VMEM((2 * E * W * SLAB, LANES), I32),
                            pltpu.VMEM((W * SLAB, LANES), I32),
                            pltpu.VMEM((tm, D), F32),
                            pltpu.SemaphoreType.DMA((3,))]),
        out_shape=jax.ShapeDtypeStruct((n, D), F32),
        compiler_params=_cparams(("arbitrary",)),
        name="ec_combine",
    )(start, pos_t, w_t, h2, ye_slabs, g)


def _pick(n, prefs):
    for p in prefs:
        if n % p == 0:
            return p
    raise ValueError(f"no tile in {prefs} divides {n}")


def _prep_weights(norm_mix, w_in, q_norm, w_uq, kv_norm, w_ukv, conv_w, conv_b, w_rg_a, b_rg_a, w_rg_x, b_rg_x,
                  rg_lambda, w_out, norm_cross, norm_mem, w_cq, w_ck, w_cv, w_co, norm_ffn, w_router,
                  w_gate, w_up, w_down, norm_final):
    swap = (jnp.arange(QK_ROPE) + QK_ROPE // 2) % QK_ROPE
    o = Q_LORA + KV_LORA
    k_r = w_in[:, o:o + QK_ROPE]
    win = jnp.concatenate([w_in[:, :o], w_in[:, o + QK_ROPE:], k_r, k_r[:, swap]], axis=1).astype(BF16)
    uq = w_uq.reshape(Q_LORA, MLA_HEADS, QK_NOPE + QK_ROPE)
    rope = uq[:, :, QK_NOPE:]
    wuq = jnp.concatenate([uq[:, :, :QK_NOPE], rope, rope[:, :, swap]], axis=2).reshape(Q_LORA, MLA_HEADS * HEAD_QK)
    wgate = jnp.concatenate([w_rg_a[0], w_rg_a[1], w_rg_x[0], w_rg_x[1]], axis=2).astype(BF16)
    bgate = jnp.concatenate([b_rg_a[0], b_rg_a[1], b_rg_x[0], b_rg_x[1]], axis=1)[:, None, :]
    row = lambda v: v.reshape(1, -1)
    return dict(
        gmix=row(norm_mix), win=win, qg=row(q_norm), wuq=wuq.astype(BF16), kvg=row(kv_norm), wukv=w_ukv.astype(BF16),
        conv_w=conv_w, conv_b=row(conv_b), wgate=wgate, bgate=bgate, lam=rg_lambda,
        wout=w_out.astype(BF16), gcross=row(norm_cross), gmem=row(norm_mem),
        wcq=w_cq.astype(BF16), wck=w_ck.astype(BF16), wcv=w_cv.astype(BF16), wco=w_co.astype(BF16),
        gffn=row(norm_ffn), wr_t=w_router.T.astype(BF16),
        wg=w_gate.astype(BF16), wu=w_up.astype(BF16), wd=w_down.astype(BF16), gfinal=row(norm_final))


def _rope_mults(S):
    inv = ROPE_THETA ** (-jnp.arange(0, QK_ROPE, 2, dtype=F32) / QK_ROPE)
    ang = jnp.arange(S, dtype=F32)[:, None] * inv[None, :]
    cos, sin = jnp.cos(ang), jnp.sin(ang)
    kmult = jnp.concatenate([cos, cos, -sin, sin], axis=1)
    scale = (QK_NOPE + QK_ROPE) ** -0.5 * 1.4426950408889634
    qmult = scale * jnp.concatenate([jnp.ones((S, QK_NOPE), F32), kmult], axis=1)
    return qmult, kmult


def _run(x, mem, W):
    B, S, D = x.shape
    n = B * S
    C = EC_CAPACITY_FACTOR * n // N_EXPERTS
    qmult, kmult = _rope_mults(S)

    q, k, v, xbr, gbr = _inproj(x, W["gmix"], W["win"], W["qg"], W["wuq"], W["kvg"], W["wukv"], qmult, kmult,
                                tm=_pick(S, (256, 128)))
    o_mla = _attention(q, k, v, tq=_pick(S, (1024, 512, 256, 128)), tk=_pick(S, (512, 256, 128)))
    o_rg = _rglru(xbr, gbr, W["conv_w"], W["conv_b"], W["wgate"], W["bgate"], W["lam"], R=_pick(S, (256, 128)))
    h1, hn = _outproj(o_mla.reshape(n, MLA_WIDTH), o_rg.reshape(n, RG_WIDTH), x.reshape(n, D), W["wout"],
                      W["gcross"], tm=_pick(n, (512, 256, 128)))
    kmem, vmem = _memkv(mem, W["gmem"], W["wck"], W["wcv"])
    h2, xn, aff_t = _cross(hn.reshape(B, S, D), h1.reshape(B, S, D), kmem, vmem, W["wcq"], W["wco"], W["gffn"],
                           W["wr_t"], tm=_pick(S, (256, 128)))

    nb = max(n // LANES, LANES)
    aff3 = jnp.pad(aff_t, ((0, 0), (0, nb * LANES - n)), constant_values=-1.0).reshape(N_EXPERTS, nb, LANES)
    idx_t, posm = _select(aff3, C)
    idx = idx_t[:, :N_EXPERTS].T.reshape(N_EXPERTS * C)
    ye = _ffn(idx, xn, W["wg"], W["wu"], W["wd"], C, tc=_pick(C, (256, 128)))

    tm = _pick(n, (256, 128))
    posm = posm.reshape(N_EXPERTS, nb * LANES)[:, :n]
    first = posm[:, ::tm]
    first = jnp.where(first < 0, first - NOT_SELECTED, first)
    start = jnp.concatenate([first.T, jnp.full((1, N_EXPERTS), C, I32)], axis=0).reshape(-1)
    y = _combine(start, posm.T, aff_t.T, h2.reshape(n, D), ye, W["gfinal"], C, tm=tm)
    return y.reshape(B, S, D)


def kernel(x_prompt, x_sample, mem_prompt, mem_sample, norm_mix, w_in, q_norm, w_uq, kv_norm, w_ukv, conv_w, conv_b,
           w_rg_a, b_rg_a, w_rg_x, b_rg_x, rg_lambda, w_out, norm_cross, norm_mem, w_cq, w_ck, w_cv, w_co, norm_ffn,
           w_router, w_gate, w_up, w_down, norm_final):
    W = _prep_weights(norm_mix[0], w_in[0], q_norm[0], w_uq[0], kv_norm[0], w_ukv[0], conv_w[0], conv_b[0],
                      w_rg_a[0], b_rg_a[0], w_rg_x[0], b_rg_x[0], rg_lambda[0], w_out[0], norm_cross[0],
                      norm_mem[0], w_cq[0], w_ck[0], w_cv[0], w_co[0], norm_ffn[0], w_router[0],
                      w_gate[0], w_up[0], w_down[0], norm_final)
    return (_run(x_prompt, mem_prompt, W), _run(x_sample, mem_sample, W))
```

```python
import functools

import jax
import jax.numpy as jnp
from jax import lax
from jax.experimental import pallas as pl
from jax.experimental.pallas import tpu as pltpu

F32 = jnp.float32
BF16 = jnp.bfloat16
I32 = jnp.int32

D_MODEL = 2048
MLA_HEADS = 8
QK_NOPE = 128
QK_ROPE = 64
V_HEAD = 128
Q_LORA = 512
KV_LORA = 512
MLA_WIDTH = MLA_HEADS * V_HEAD
RG_WIDTH = D_MODEL - MLA_WIDTH
RG_BLOCKS = 8
RG_BLOCK = RG_WIDTH // RG_BLOCKS
LRU_C = 8.0
MEM_TOKENS = 256
X_HEADS = 4
X_HEAD = D_MODEL // X_HEADS
N_EXPERTS = 16
EC_CAPACITY_FACTOR = 2
EXPERT_FF = 1408
ROPE_THETA = 10000.0
EPS = 1e-6

LANES = 128
HEAD_QK = 2 * LANES
VMEM_LIMIT = 56 * 1024 * 1024


def _cparams(sem, vmem=VMEM_LIMIT):
    return pltpu.CompilerParams(dimension_semantics=sem, vmem_limit_bytes=vmem)


def _rms(x, g):
    ms = jnp.mean(x * x, axis=-1, keepdims=True)
    return x * lax.rsqrt(ms + EPS) * g


def _sigmoid(x):
    return 0.5 * jnp.tanh(0.5 * x) + 0.5


def _dot(a, b):
    return jnp.dot(a, b, preferred_element_type=F32)


def _dot_nt(a, b):
    return lax.dot_general(a, b, (((1,), (1,)), ((), ())), preferred_element_type=F32)


SLAB = D_MODEL // (2 * LANES)
HI_MASK = -65536


def _store_slabs(ref, row0, m, y):
    for j in range(SLAB):
        lo = pltpu.bitcast(y[:, j * LANES:(j + 1) * LANES].astype(BF16).astype(F32), I32)
        hi = pltpu.bitcast(y[:, (j + SLAB) * LANES:(j + SLAB + 1) * LANES].astype(BF16).astype(F32), I32)
        ref[pl.ds(row0 * SLAB + j, m, stride=SLAB), :] = (hi & HI_MASK) | lax.shift_right_logical(lo, 16)


def _load_slabs(ref, row0, m):
    lo, hi = [], []
    for j in range(SLAB):
        w = ref[pl.ds(row0 * SLAB + j, m, stride=SLAB), :]
        lo.append(pltpu.bitcast(lax.shift_left(w, 16), F32).astype(BF16))
        hi.append(pltpu.bitcast(w & HI_MASK, F32).astype(BF16))
    return jnp.concatenate(lo + hi, axis=1)


def _inproj_kernel(x_ref, gmix_ref, win_ref, qg_ref, wuq_ref, kvg_ref, wukv_ref,
                   qmult_ref, kmult_ref, q_ref, k_ref, v_ref, xbr_ref, gbr_ref):
    hn = _rms(x_ref[0], gmix_ref[...]).astype(BF16)
    cq = _dot(hn, win_ref[:, 0:Q_LORA])
    ckv = _dot(hn, win_ref[:, Q_LORA:Q_LORA + KV_LORA])
    o = Q_LORA + KV_LORA
    xbr_ref[0] = _dot(hn, win_ref[:, o:o + RG_WIDTH])
    gbr_ref[0] = _dot(hn, win_ref[:, o + RG_WIDTH:o + 2 * RG_WIDTH])
    kr = _dot(hn, win_ref[:, o + 2 * RG_WIDTH:o + 2 * RG_WIDTH + LANES])
    tk = kr * kmult_ref[...]
    rkk = (tk + pltpu.roll(tk, LANES // 2, 1)).astype(BF16)

    q = _dot(_rms(cq, qg_ref[...]).astype(BF16), wuq_ref[...])
    kv = _dot(_rms(ckv, kvg_ref[...]).astype(BF16), wukv_ref[...])
    qm = qmult_ref[...]
    for h in range(MLA_HEADS):
        c = h * HEAD_QK
        q_ref[0, h] = (q[:, c:c + HEAD_QK] * qm).astype(BF16)
        k_ref[0, h, :, 0:LANES] = kv[:, c:c + LANES].astype(BF16)
        k_ref[0, h, :, LANES:HEAD_QK] = rkk
        v_ref[0, h, :, 0:LANES] = kv[:, c + LANES:c + HEAD_QK].astype(BF16)
        v_ref[0, h, :, LANES:HEAD_QK] = jnp.ones((kv.shape[0], LANES), BF16)


def _inproj(x, gmix, win, qg, wuq, kvg, wukv, qmult, kmult, tm):
    B, S, D = x.shape
    full = lambda a: pl.BlockSpec(a.shape, lambda b, i: (0,) * a.ndim)
    return pl.pallas_call(
        _inproj_kernel,
        grid=(B, S // tm),
        in_specs=[pl.BlockSpec((1, tm, D), lambda b, i: (b, i, 0)),
                  full(gmix), full(win), full(qg), full(wuq), full(kvg), full(wukv),
                  pl.BlockSpec((tm, HEAD_QK), lambda b, i: (i, 0)),
                  pl.BlockSpec((tm, LANES), lambda b, i: (i, 0))],
        out_specs=[pl.BlockSpec((1, MLA_HEADS, tm, HEAD_QK), lambda b, i: (b, 0, i, 0)),
                   pl.BlockSpec((1, MLA_HEADS, tm, HEAD_QK), lambda b, i: (b, 0, i, 0)),
                   pl.BlockSpec((1, MLA_HEADS, tm, 2 * V_HEAD), lambda b, i: (b, 0, i, 0)),
                   pl.BlockSpec((1, tm, RG_WIDTH), lambda b, i: (b, i, 0)),
                   pl.BlockSpec((1, tm, RG_WIDTH), lambda b, i: (b, i, 0))],
        out_shape=[jax.ShapeDtypeStruct((B, MLA_HEADS, S, HEAD_QK), BF16),
                   jax.ShapeDtypeStruct((B, MLA_HEADS, S, HEAD_QK), BF16),
                   jax.ShapeDtypeStruct((B, MLA_HEADS, S, 2 * V_HEAD), BF16),
                   jax.ShapeDtypeStruct((B, S, RG_WIDTH), F32),
                   jax.ShapeDtypeStruct((B, S, RG_WIDTH), F32)],
        compiler_params=_cparams(("parallel", "parallel")),
        name="inproj",
    )(x, gmix, win, qg, wuq, kvg, wukv, qmult, kmult)


def _attn_kernel(q_ref, k_ref, v_ref, o_ref, *, tk):
    q = q_ref[0, 0]
    tq = q.shape[0]
    nk = k_ref.shape[2] // tk

    def body(j, carry):
        m, acc = carry
        off = pl.multiple_of(j * tk, tk)
        s = _dot_nt(q, k_ref[0, 0, pl.ds(off, tk), :])
        m_new = jnp.maximum(m, jnp.max(s, axis=1, keepdims=True))
        alpha = jnp.exp2(m - m_new)
        p = jnp.exp2(s - m_new)
        acc = alpha * acc + _dot(p.astype(BF16), v_ref[0, 0, pl.ds(off, tk), :])
        return m_new, acc

    init = (jnp.full((tq, 1), -jnp.inf, F32), jnp.zeros((tq, 2 * V_HEAD), F32))
    _, acc = lax.fori_loop(0, nk, body, init, unroll=True)
    o_ref[0] = (acc[:, :V_HEAD] / acc[:, V_HEAD:]).astype(BF16)


def _attention(q, k, v, tq, tk):
    B, H, S, _ = q.shape
    return pl.pallas_call(
        functools.partial(_attn_kernel, tk=tk),
        grid=(B, H, S // tq),
        in_specs=[pl.BlockSpec((1, 1, tq, HEAD_QK), lambda b, h, i: (b, h, i, 0)),
                  pl.BlockSpec((1, 1, S, HEAD_QK), lambda b, h, i: (b, h, 0, 0)),
                  pl.BlockSpec((1, 1, S, 2 * V_HEAD), lambda b, h, i: (b, h, 0, 0))],
        out_specs=pl.BlockSpec((1, tq, V_HEAD), lambda b, h, i: (b, i, h)),
        out_shape=jax.ShapeDtypeStruct((B, S, H * V_HEAD), BF16),
        compiler_params=_cparams(("parallel", "parallel", "parallel")),
        name="mla_attention",
    )(q, k, v)


def _rg_kernel(x_ref, g_ref, cw_ref, cb_ref, wg_ref, bg_ref, lam_ref, o_ref,
               hf_ref, ab_ref, ub_ref, *, R):
    S = x_ref.shape[1]
    nch = S // R
    W = RG_BLOCK
    cw = cw_ref[...]
    cb = cb_ref[...]
    ls = LRU_C * jax.nn.log_sigmoid(lam_ref[...])
    rows = lax.broadcasted_iota(I32, (R, W), 0)
    next_rows = R + 16

    def gates(c):
        r0 = pl.multiple_of(c * R, R)
        cur = x_ref[0, pl.ds(r0, R), :]
        prev8 = x_ref[0, pl.ds(pl.multiple_of(jnp.maximum(r0 - 8, 0), 8), 8), :]
        prev8 = jnp.where(c > 0, prev8, 0.0)
        next8 = x_ref[0, pl.ds(pl.multiple_of(jnp.minimum(r0 + R, S - 8), 8), 8), :]
        next8 = jnp.where(c < nch - 1, next8, 0.0)
        ext = jnp.concatenate([prev8, cur, next8], axis=0)
        xm1 = pltpu.roll(ext, 1, 0)[8:8 + R]
        xp1 = pltpu.roll(ext, next_rows - 1, 0)[8:8 + R]
        xp2 = pltpu.roll(ext, next_rows - 2, 0)[8:8 + R]
        xc = cb + xm1 * cw[0:1] + cur * cw[1:2] + xp1 * cw[2:3] + xp2 * cw[3:4]
        sg = _sigmoid(_dot(xc.astype(BF16), wg_ref[0]) + bg_ref[0])
        out = []
        for n in range(2):
            log_a = sg[:, n * W:(n + 1) * W] * ls[n:n + 1]
            a = jnp.exp(log_a)
            t = jnp.tanh(log_a)
            p = -2.0 * t
            root = jnp.where(p > 0.0, p * lax.rsqrt(p * (1.0 - t)), 0.0)
            u = root * sg[:, (2 + n) * W:(3 + n) * W] * xc
            out.append((a, u))
        return out

    def scan(a, u, reverse):
        d = 1
        while d < 8:
            if reverse:
                keep = rows < R - d
                sh = R - d
            else:
                keep = rows >= d
                sh = d
            a_s = jnp.where(keep, pltpu.roll(a, sh, 0), 1.0)
            u_s = jnp.where(keep, pltpu.roll(u, sh, 0), 0.0)
            u = a * u_s + u
            a = a * a_s
            d *= 2
        while d < R:
            if reverse:
                u = jnp.concatenate([a[:R - d] * u[d:] + u[:R - d], u[R - d:]], axis=0)
                a = jnp.concatenate([a[:R - d] * a[d:], a[R - d:]], axis=0)
            else:
                u = jnp.concatenate([u[:d], a[d:] * u[:R - d] + u[d:]], axis=0)
                a = jnp.concatenate([a[:d], a[d:] * a[:R - d]], axis=0)
            d *= 2
        return a, u

    def fwd(c, carry):
        r0 = pl.multiple_of(c * R, R)
        (a_f, u_f), (a_b, u_b) = gates(c)
        ab_ref[pl.ds(r0, R), :] = a_b
        ub_ref[pl.ds(r0, R), :] = u_b
        acum, hloc = scan(a_f, u_f, False)
        h = acum * carry + hloc
        hf_ref[pl.ds(r0, R), :] = h
        return h[R - 1:R, :]

    lax.fori_loop(0, nch, fwd, jnp.zeros((1, W), F32))

    def bwd(i, carry):
        c = nch - 1 - i
        r0 = pl.multiple_of(c * R, R)
        acum, hloc = scan(ab_ref[pl.ds(r0, R), :], ub_ref[pl.ds(r0, R), :], True)
        h = acum * carry + hloc
        gate = jax.nn.gelu(g_ref[0, pl.ds(r0, R), :], approximate=True)
        o_ref[0, pl.ds(r0, R), :] = ((hf_ref[pl.ds(r0, R), :] + h) * gate).astype(BF16)
        return h[0:1, :]

    lax.fori_loop(0, nch, bwd, jnp.zeros((1, W), F32))


def _rglru(xbr, gbr, conv_w, conv_b, wgate, bgate, lam, R):
    B, S, _ = xbr.shape
    W = RG_BLOCK
    return pl.pallas_call(
        functools.partial(_rg_kernel, R=R),
        grid=(B, RG_BLOCKS),
        in_specs=[pl.BlockSpec((1, S, W), lambda b, g: (b, 0, g)),
                  pl.BlockSpec((1, S, W), lambda b, g: (b, 0, g)),
                  pl.BlockSpec((4, W), lambda b, g: (0, g)),
                  pl.BlockSpec((1, W), lambda b, g: (0, g)),
                  pl.BlockSpec((1, W, 4 * W), lambda b, g: (g, 0, 0)),
                  pl.BlockSpec((1, 1, 4 * W), lambda b, g: (g, 0, 0)),
                  pl.BlockSpec((2, W), lambda b, g: (0, g))],
        out_specs=pl.BlockSpec((1, S, W), lambda b, g: (b, 0, g)),
        out_shape=jax.ShapeDtypeStruct((B, S, RG_WIDTH), BF16),
        scratch_shapes=[pltpu.VMEM((S, W), F32), pltpu.VMEM((S, W), F32), pltpu.VMEM((S, W), F32)],
        compiler_params=_cparams(("parallel", "parallel")),
        name="rglru",
    )(xbr, gbr, conv_w, conv_b, wgate, bgate, lam)


def _outproj_kernel(om_ref, og_ref, x_ref, wout_ref, g_ref, h_ref, hn_ref):
    h = x_ref[...] + _dot(om_ref[...], wout_ref[0:MLA_WIDTH, :]) + _dot(og_ref[...], wout_ref[MLA_WIDTH:, :])
    h_ref[...] = h
    hn_ref[...] = _rms(h, g_ref[...]).astype(BF16)


def _outproj(o_mla, o_rg, x, wout, g, tm):
    M, D = x.shape
    return pl.pallas_call(
        _outproj_kernel,
        grid=(M // tm,),
        in_specs=[pl.BlockSpec((tm, MLA_WIDTH), lambda i: (i, 0)),
                  pl.BlockSpec((tm, RG_WIDTH), lambda i: (i, 0)),
                  pl.BlockSpec((tm, D), lambda i: (i, 0)),
                  pl.BlockSpec((D, D), lambda i: (0, 0)),
                  pl.BlockSpec((1, D), lambda i: (0, 0))],
        out_specs=[pl.BlockSpec((tm, D), lambda i: (i, 0)), pl.BlockSpec((tm, D), lambda i: (i, 0))],
        out_shape=[jax.ShapeDtypeStruct((M, D), F32), jax.ShapeDtypeStruct((M, D), BF16)],
        compiler_params=_cparams(("parallel",)),
        name="outproj",
    )(o_mla, o_rg, x, wout, g)


def _memkv_kernel(mem_ref, g_ref, wk_ref, wv_ref, k_ref, v_ref):
    mn = _rms(mem_ref[0], g_ref[...]).astype(BF16)
    k_ref[0] = _dot(mn, wk_ref[...]).astype(BF16)
    v_ref[0] = _dot(mn, wv_ref[...]).astype(BF16)


def _memkv(mem, g, wk, wv):
    B, Mt, D = mem.shape
    return pl.pallas_call(
        _memkv_kernel,
        grid=(B,),
        in_specs=[pl.BlockSpec((1, Mt, D), lambda b: (b, 0, 0)),
                  pl.BlockSpec((1, D), lambda b: (0, 0)),
                  pl.BlockSpec((D, D), lambda b: (0, 0)),
                  pl.BlockSpec((D, D), lambda b: (0, 0))],
        out_specs=[pl.BlockSpec((1, Mt, D), lambda b: (b, 0, 0)), pl.BlockSpec((1, Mt, D), lambda b: (b, 0, 0))],
        out_shape=[jax.ShapeDtypeStruct((B, Mt, D), BF16), jax.ShapeDtypeStruct((B, Mt, D), BF16)],
        compiler_params=_cparams(("parallel",)),
        name="memkv",
    )(mem, g, wk, wv)


def _cross_kernel(hn_ref, h_ref, k_ref, v_ref, wq_ref, wo_ref, g_ref, wr_ref, h2_ref, xn_ref, aff_ref):
    q = _dot(hn_ref[0], wq_ref[...]).astype(BF16)
    outs = []
    for hd in range(X_HEADS):
        c = hd * X_HEAD
        s = _dot_nt(q[:, c:c + X_HEAD], k_ref[0, :, c:c + X_HEAD]) * (X_HEAD ** -0.5)
        e = jnp.exp(s - jnp.max(s, axis=1, keepdims=True))
        p = e * (1.0 / jnp.sum(e, axis=1, keepdims=True))
        outs.append(_dot(p.astype(BF16), v_ref[0, :, c:c + X_HEAD]).astype(BF16))
    h2 = h_ref[0] + _dot(jnp.concatenate(outs, axis=1), wo_ref[...])
    h2_ref[0] = h2
    xn = _rms(h2, g_ref[...])
    _store_slabs(xn_ref, 0, xn.shape[0], xn)
    logits = _dot_nt(wr_ref[...], xn.astype(BF16))
    e = jnp.exp(logits - jnp.max(logits, axis=0, keepdims=True))
    aff_ref[...] = e / jnp.sum(e, axis=0, keepdims=True)


def _cross(hn, h1, kmem, vmem, wq, wo, g, wr_t, tm):
    B, S, D = h1.shape
    nt = S // tm
    return pl.pallas_call(
        _cross_kernel,
        grid=(B, nt),
        in_specs=[pl.BlockSpec((1, tm, D), lambda b, i: (b, i, 0)),
                  pl.BlockSpec((1, tm, D), lambda b, i: (b, i, 0)),
                  pl.BlockSpec((1, MEM_TOKENS, D), lambda b, i: (b, 0, 0)),
                  pl.BlockSpec((1, MEM_TOKENS, D), lambda b, i: (b, 0, 0)),
                  pl.BlockSpec((D, D), lambda b, i: (0, 0)),
                  pl.BlockSpec((D, D), lambda b, i: (0, 0)),
                  pl.BlockSpec((1, D), lambda b, i: (0, 0)),
                  pl.BlockSpec((N_EXPERTS, D), lambda b, i: (0, 0))],
        out_specs=[pl.BlockSpec((1, tm, D), lambda b, i: (b, i, 0)),
                   pl.BlockSpec((tm * SLAB, LANES), lambda b, i: (b * nt + i, 0)),
                   pl.BlockSpec((N_EXPERTS, tm), lambda b, i: (0, b * nt + i))],
        out_shape=[jax.ShapeDtypeStruct((B, S, D), F32),
                   jax.ShapeDtypeStruct((B * S * SLAB, LANES), I32),
                   jax.ShapeDtypeStruct((N_EXPERTS, B * S), F32)],
        compiler_params=_cparams(("parallel", "parallel")),
        name="cross_router",
    )(hn, h1, kmem, vmem, wq, wo, g, wr_t)


NOT_SELECTED = -(1 << 20)


def _select_kernel(aff_ref, idx_ref, pos_ref, *, C, CB):
    nb = aff_ref.shape[1]
    L = LANES
    li = lax.broadcasted_iota(I32, (L, L), 0)
    lj = lax.broadcasted_iota(I32, (L, L), 1)
    upper_l = (li <= lj).astype(BF16)
    ones_l = jnp.ones((L, L), BF16)
    bi = lax.broadcasted_iota(I32, (nb, nb), 0)
    bj = lax.broadcasted_iota(I32, (nb, nb), 1)
    lower_strict = (bj < bi).astype(BF16)
    upper_b = (bi <= bj).astype(BF16)
    blk_id = lax.broadcasted_iota(I32, (nb, L), 0).astype(F32)
    lane = lax.broadcasted_iota(I32, (CB, L), 1)

    idx_ref[...] = jnp.zeros_like(idx_ref)

    def count(m):
        return jnp.sum(jnp.sum(m.astype(I32), axis=1, keepdims=True), axis=0, keepdims=True)

    def prefix(m_b):
        within = _dot(m_b, upper_l)
        tot = _dot(m_b, ones_l)
        t_excl = _dot(lower_strict, tot.astype(BF16))
        return within, tot, t_excl

    def per_expert(e, _):
        a = aff_ref[e]
        bits = pltpu.bitcast(a, I32)

        def bisect(i, thr):
            cand = thr | jnp.left_shift(jnp.int32(1), 30 - i)
            return jnp.where(count(bits >= cand) >= C, cand, thr)

        thr = lax.fori_loop(0, 31, bisect, jnp.zeros((1, 1), I32))
        gt = bits > thr
        eq = bits == thr
        need = (C - count(gt)).astype(F32)
        e_within, _, e_excl = prefix(eq.astype(BF16))
        eq_rank = e_within - eq.astype(F32) + e_excl
        mask = gt | (eq & (eq_rank < need))
        mask_f = mask.astype(F32)
        within, tot, t_excl = prefix(mask.astype(BF16))
        pos = (within - mask_f + t_excl).astype(I32)
        pos_ref[e] = jnp.where(mask, pos, pos + NOT_SELECTED)

        tot_row = _dot_nt(ones_l, mask.astype(BF16))
        t_incl_row = _dot(tot_row.astype(BF16), upper_b)[0:1, :]
        t_excl_row = t_incl_row - tot_row[0:1, :]
        hi = jnp.floor(t_excl * (1.0 / 256.0))
        rhs = jnp.concatenate([within, hi, t_excl - 256.0 * hi, blk_id], axis=1).astype(BF16)

        def per_chunk(ci, _):
            c0 = pl.multiple_of(ci * CB, CB)
            c_b = (c0 + lax.broadcasted_iota(I32, (CB, nb), 0)).astype(F32)
            onehot = ((t_excl_row <= c_b) & (c_b < t_incl_row)).astype(BF16)
            g = _dot(onehot, rhs)
            c_l = (c0 + lax.broadcasted_iota(I32, (CB, L), 0)).astype(F32)
            inside = (g[:, 0:L] + (256.0 * g[:, L:2 * L] + g[:, 2 * L:3 * L])) <= c_l
            val = (g[:, 3 * L:4 * L] * float(L) + _dot(inside.astype(BF16), ones_l)).astype(I32)
            idx_ref[pl.ds(c0, CB), :] = jnp.where(lane == e, val, idx_ref[pl.ds(c0, CB), :])
            return 0

        lax.fori_loop(0, C // CB, per_chunk, 0)
        return 0

    lax.fori_loop(0, N_EXPERTS, per_expert, 0)


def _select(aff3, C):
    E, nb, L = aff3.shape
    CB = min(C, 512)
    return pl.pallas_call(
        functools.partial(_select_kernel, C=C, CB=CB),
        out_shape=[jax.ShapeDtypeStruct((C, L), I32),
                   jax.ShapeDtypeStruct((E, nb, L), I32)],
        compiler_params=pltpu.CompilerParams(vmem_limit_bytes=VMEM_LIMIT),
        name="ec_select",
    )(aff3)


def _ffn_kernel(idx_ref, x_hbm, wg_ref, wu_ref, wd_ref, o_ref, xbuf, sem, *, tc, nsteps):
    step = pl.program_id(0) * pl.num_programs(1) + pl.program_id(1)
    slot = lax.rem(step, 2)

    def fetch(s, dst_slot):
        for u in range(tc):
            tok = idx_ref[s * tc + u]
            pltpu.make_async_copy(x_hbm.at[pl.ds(tok * SLAB, SLAB)],
                                  xbuf.at[pl.ds((dst_slot * tc + u) * SLAB, SLAB)], sem.at[dst_slot]).start(priority=u % 2)

    def wait(dst_slot):
        pltpu.make_async_copy(x_hbm.at[pl.ds(0, tc * SLAB)],
                              xbuf.at[pl.ds(dst_slot * tc * SLAB, tc * SLAB)], sem.at[dst_slot]).wait()

    @pl.when(step == 0)
    def _():
        fetch(0, 0)

    fetch(jnp.minimum(step + 1, nsteps - 1), 1 - slot)
    wait(slot)
    x = _load_slabs(xbuf, slot * tc, tc)
    gate = _dot(x, wg_ref[0])
    hid = gate * _sigmoid(gate) * _dot(x, wu_ref[0])
    _store_slabs(o_ref, 0, tc, _dot(hid.astype(BF16), wd_ref[0]))

    @pl.when(step == nsteps - 1)
    def _():
        wait(1 - slot)


def _ffn(idx, xn_slabs, wg, wu, wd, C, tc):
    E, D, F = wg.shape
    nt = C // tc
    return pl.pallas_call(
        functools.partial(_ffn_kernel, tc=tc, nsteps=E * nt),
        grid_spec=pltpu.PrefetchScalarGridSpec(
            num_scalar_prefetch=1,
            grid=(E, nt),
            in_specs=[pl.BlockSpec(memory_space=pl.ANY),
                      pl.BlockSpec((1, D, F), lambda e, i, idx: (e, 0, 0)),
                      pl.BlockSpec((1, D, F), lambda e, i, idx: (e, 0, 0)),
                      pl.BlockSpec((1, F, D), lambda e, i, idx: (e, 0, 0))],
            out_specs=pl.BlockSpec((tc * SLAB, LANES), lambda e, i, idx: (e * nt + i, 0)),
            scratch_shapes=[pltpu.VMEM((2 * tc * SLAB, LANES), I32), pltpu.SemaphoreType.DMA((2,))]),
        out_shape=jax.ShapeDtypeStruct((E * C * SLAB, LANES), I32),
        compiler_params=_cparams(("arbitrary", "arbitrary")),
        name="ec_ffn",
    )(idx, xn_slabs, wg, wu, wd)


COMBINE_W = 64


def _split_bf16(g):
    hi = g.astype(BF16)
    return hi, (g - hi.astype(F32)).astype(BF16)


def _combine_kernel(start_ref, pos_ref, w_ref, h_ref, ye_hbm, g_ref, o_ref, ybuf, ybuf2, acc_ref, sem,
                    *, tm, C, nt):
    E, W = N_EXPERTS, COMBINE_W
    i = pl.program_id(0)
    slot = lax.rem(i, 2)

    def first_slot(tile, e):
        return jnp.minimum(start_ref[tile * E + e], C - W)

    def fetch(tile, dst_slot):
        for e in range(E):
            pltpu.make_async_copy(ye_hbm.at[pl.ds((e * C + first_slot(tile, e)) * SLAB, W * SLAB)],
                                  ybuf.at[pl.ds((dst_slot * E + e) * W * SLAB, W * SLAB)], sem.at[dst_slot]).start()

    def wait(dst_slot):
        pltpu.make_async_copy(ye_hbm.at[pl.ds(0, E * W * SLAB)],
                              ybuf.at[pl.ds(dst_slot * E * W * SLAB, E * W * SLAB)], sem.at[dst_slot]).wait()

    @pl.when(i == 0)
    def _():
        fetch(0, 0)

    fetch(jnp.minimum(i + 1, nt - 1), 1 - slot)
    wait(slot)

    pos = pos_ref[...]
    w = w_ref[...]
    lane = lax.broadcasted_iota(I32, (tm, 2 * W), 1)
    pieces = []
    for p in range(E // 2):
        e0, e1 = 2 * p, 2 * p + 1
        k0 = pos[:, e0:e0 + 1] - first_slot(i, e0)
        k1 = pos[:, e1:e1 + 1] - first_slot(i, e1) + W
        pieces.append(jnp.where((lane == k0) & (lane < W), w[:, e0:e0 + 1],
                                jnp.where((lane == k1) & (lane >= W), w[:, e1:e1 + 1], 0.0)))
    g_hi, g_lo = _split_bf16(jnp.concatenate(pieces, axis=1))
    y = _load_slabs(ybuf, slot * E * W, E * W)
    acc_ref[...] = h_ref[...] + _dot(g_hi, y) + _dot(g_lo, y)

    lane_w = lax.broadcasted_iota(I32, (tm, W), 1)
    for e in range(E):
        s0 = first_slot(i, e)
        n_extra = jnp.maximum((start_ref[(i + 1) * E + e] - s0 + W - 1) // W - 1, 0)

        def extra(kk, _, e=e, s0=s0):
            lo_slot = s0 + (kk + 1) * W
            s2 = jnp.minimum(lo_slot, C - W)
            cp = pltpu.make_async_copy(ye_hbm.at[pl.ds((e * C + s2) * SLAB, W * SLAB)], ybuf2, sem.at[2])
            cp.start()
            cp.wait()
            pe = pos[:, e:e + 1]
            gx_hi, gx_lo = _split_bf16(jnp.where((lane_w == pe - s2) & (pe >= lo_slot), w[:, e:e + 1], 0.0))
            y2 = _load_slabs(ybuf2, 0, W)
            acc_ref[...] += _dot(gx_hi, y2) + _dot(gx_lo, y2)
            return 0

        lax.fori_loop(0, n_extra, extra, 0)

    o_ref[...] = _rms(acc_ref[...], g_ref[...])

    @pl.when(i == nt - 1)
    def _():
        wait(1 - slot)


def _combine(start, pos_t, w_t, h2, ye_slabs, g, C, tm):
    n, D = h2.shape
    nt = n // tm
    E, W = N_EXPERTS, COMBINE_W
    assert C >= W
    return pl.pallas_call(
        functools.partial(_combine_kernel, tm=tm, C=C, nt=nt),
        grid_spec=pltpu.PrefetchScalarGridSpec(
            num_scalar_prefetch=1,
            grid=(nt,),
            in_specs=[pl.BlockSpec((tm, E), lambda i, st: (i, 0)),
                      pl.BlockSpec((tm, E), lambda i, st: (i, 0)),
                      pl.BlockSpec((tm, D), lambda i, st: (i, 0)),
                      pl.BlockSpec(memory_space=pl.ANY),
                      pl.BlockSpec((1, D), lambda i, st: (0, 0))],
            out_specs=pl.BlockSpec((tm, D), lambda i, st: (i, 0)),
            scratch_shapes=[pltpu.VMEM((2 * E * W * SLAB, LANES), I32),
                            pltpu.VMEM((W * SLAB, LANES), I32),
                            pltpu.VMEM((tm, D), F32),
                            pltpu.SemaphoreType.DMA((3,))]),
        out_shape=jax.ShapeDtypeStruct((n, D), F32),
        compiler_params=_cparams(("arbitrary",)),
        name="ec_combine",
    )(start, pos_t, w_t, h2, ye_slabs, g)


def _pick(n, prefs):
    for p in prefs:
        if n % p == 0:
            return p
    raise ValueError(f"no tile in {prefs} divides {n}")


def _prep_weights(norm_mix, w_in, q_norm, w_uq, kv_norm, w_ukv, conv_w, conv_b, w_rg_a, b_rg_a, w_rg_x, b_rg_x,
                  rg_lambda, w_out, norm_cross, norm_mem, w_cq, w_ck, w_cv, w_co, norm_ffn, w_router,
                  w_gate, w_up, w_down, norm_final):
    swap = (jnp.arange(QK_ROPE) + QK_ROPE // 2) % QK_ROPE
    o = Q_LORA + KV_LORA
    k_r = w_in[:, o:o + QK_ROPE]
    win = jnp.concatenate([w_in[:, :o], w_in[:, o + QK_ROPE:], k_r, k_r[:, swap]], axis=1).astype(BF16)
    uq = w_uq.reshape(Q_LORA, MLA_HEADS, QK_NOPE + QK_ROPE)
    rope = uq[:, :, QK_NOPE:]
    wuq = jnp.concatenate([uq[:, :, :QK_NOPE], rope, rope[:, :, swap]], axis=2).reshape(Q_LORA, MLA_HEADS * HEAD_QK)
    wgate = jnp.concatenate([w_rg_a[0], w_rg_a[1], w_rg_x[0], w_rg_x[1]], axis=2).astype(BF16)
    bgate = jnp.concatenate([b_rg_a[0], b_rg_a[1], b_rg_x[0], b_rg_x[1]], axis=1)[:, None, :]
    row = lambda v: v.reshape(1, -1)
    return dict(
        gmix=row(norm_mix), win=win, qg=row(q_norm), wuq=wuq.astype(BF16), kvg=row(kv_norm), wukv=w_ukv.astype(BF16),
        conv_w=conv_w, conv_b=row(conv_b), wgate=wgate, bgate=bgate, lam=rg_lambda,
        wout=w_out.astype(BF16), gcross=row(norm_cross), gmem=row(norm_mem),
        wcq=w_cq.astype(BF16), wck=w_ck.astype(BF16), wcv=w_cv.astype(BF16), wco=w_co.astype(BF16),
        gffn=row(norm_ffn), wr_t=w_router.T.astype(BF16),
        wg=w_gate.astype(BF16), wu=w_up.astype(BF16), wd=w_down.astype(BF16), gfinal=row(norm_final))


def _rope_mults(S):
    inv = ROPE_THETA ** (-jnp.arange(0, QK_ROPE, 2, dtype=F32) / QK_ROPE)
    ang = jnp.arange(S, dtype=F32)[:, None] * inv[None, :]
    cos, sin = jnp.cos(ang), jnp.sin(ang)
    kmult = jnp.concatenate([cos, cos, -sin, sin], axis=1)
    scale = (QK_NOPE + QK_ROPE) ** -0.5 * 1.4426950408889634
    qmult = scale * jnp.concatenate([jnp.ones((S, QK_NOPE), F32), kmult], axis=1)
    return qmult, kmult


def _run(x, mem, W):
    B, S, D = x.shape
    n = B * S
    C = EC_CAPACITY_FACTOR * n // N_EXPERTS
    qmult, kmult = _rope_mults(S)

    q, k, v, xbr, gbr = _inproj(x, W["gmix"], W["win"], W["qg"], W["wuq"], W["kvg"], W["wukv"], qmult, kmult,
                                tm=_pick(S, (256, 128)))
    o_mla = _attention(q, k, v, tq=_pick(S, (1024, 512, 256, 128)), tk=_pick(S, (512, 256, 128)))
    o_rg = _rglru(xbr, gbr, W["conv_w"], W["conv_b"], W["wgate"], W["bgate"], W["lam"], R=_pick(S, (256, 128)))
    h1, hn = _outproj(o_mla.reshape(n, MLA_WIDTH), o_rg.reshape(n, RG_WIDTH), x.reshape(n, D), W["wout"],
                      W["gcross"], tm=_pick(n, (512, 256, 128)))
    kmem, vmem = _memkv(mem, W["gmem"], W["wck"], W["wcv"])
    h2, xn, aff_t = _cross(hn.reshape(B, S, D), h1.reshape(B, S, D), kmem, vmem, W["wcq"], W["wco"], W["gffn"],
                           W["wr_t"], tm=_pick(S, (256, 128)))

    nb = max(n // LANES, LANES)
    aff3 = jnp.pad(aff_t, ((0, 0), (0, nb * LANES - n)), constant_values=-1.0).reshape(N_EXPERTS, nb, LANES)
    idx_t, posm = _select(aff3, C)
    idx = idx_t[:, :N_EXPERTS].T.reshape(N_EXPERTS * C)
    ye = _ffn(idx, xn, W["wg"], W["wu"], W["wd"], C, tc=_pick(C, (256, 128)))

    tm = _pick(n, (256, 128))
    posm = posm.reshape(N_EXPERTS, nb * LANES)[:, :n]
    first = posm[:, ::tm]
    first = jnp.where(first < 0, first - NOT_SELECTED, first)
    start = jnp.concatenate([first.T, jnp.full((1, N_EXPERTS), C, I32)], axis=0).reshape(-1)
    y = _combine(start, posm.T, aff_t.T, h2.reshape(n, D), ye, W["gfinal"], C, tm=tm)
    return y.reshape(B, S, D)


def kernel(x_prompt, x_sample, mem_prompt, mem_sample, norm_mix, w_in, q_norm, w_uq, kv_norm, w_ukv, conv_w, conv_b,
           w_rg_a, b_rg_a, w_rg_x, b_rg_x, rg_lambda, w_out, norm_cross, norm_mem, w_cq, w_ck, w_cv, w_co, norm_ffn,
           w_router, w_gate, w_up, w_down, norm_final):
    W = _prep_weights(norm_mix[0], w_in[0], q_norm[0], w_uq[0], kv_norm[0], w_ukv[0], conv_w[0], conv_b[0],
                      w_rg_a[0], b_rg_a[0], w_rg_x[0], b_rg_x[0], rg_lambda[0], w_out[0], norm_cross[0],
                      norm_mem[0], w_cq[0], w_ck[0], w_cv[0], w_co[0], norm_ffn[0], w_router[0],
                      w_gate[0], w_up[0], w_down[0], norm_final)
    return (_run(x_prompt, mem_prompt, W), _run(x_sample, mem_sample, W))
```
